```python
import math
import jax, jax.numpy as jnp
from jax import lax
import numpy as np

D_MODEL = 2048
BATCH = 2
SEQ = 16384
DEPTH = 2

D_MIX = D_MODEL
NSA_HEADS = 16
NSA_KV_HEADS = 4
NSA_HEAD_DIM = 64
NSA_GROUP = NSA_HEADS // NSA_KV_HEADS
CMP_LEN = 32
CMP_STRIDE = 16
CMP_HIDDEN = 128
SEL_BLOCK = 64
N_SEL = 16
WINDOW = 512
Q_BLOCK = 128
ML_HEADS = 4
ML_HEAD_DIM = 128
ML_CHUNK = 64
ML_CONV = 4
RW_HEADS = 8
RW_HEAD_DIM = 64
RW_DECAY_RANK = 64
RW_AICL_RANK = 64
RW_GATE_RANK = 128
D_FF = 5632
REL_BUCKETS = 32
REL_MAX_DIST = 128

NORM_EPS = 1e-6
RW_LN_EPS = 64e-5
MASK_NEG = -1e30

NSA_W = NSA_HEADS * NSA_HEAD_DIM
NSA_KV_W = NSA_KV_HEADS * NSA_HEAD_DIM
ML_W = ML_HEADS * ML_HEAD_DIM
RW_W = RW_HEADS * RW_HEAD_DIM
NSA_COLS = NSA_W + 6 * NSA_KV_W + 3 * NSA_HEADS
ML_COLS = 4 * ML_W + 2 * ML_HEADS
RW_COLS = 3 * RW_W + RW_DECAY_RANK + RW_AICL_RANK + RW_GATE_RANK
N_IN = NSA_COLS + ML_COLS + RW_COLS

kernel_name = "hybrid_nsa_mlstm_rwkv7_macaron"


def split_cols(z, sizes):
    return jnp.split(z, np.cumsum(sizes)[:-1].tolist(), axis=-1)


def rmsnorm(x, g):
    x32 = x.astype(jnp.float32)
    y = x32 * lax.rsqrt(jnp.mean(x32 * x32, axis=-1, keepdims=True) + NORM_EPS)
    return (y * g.astype(jnp.float32)).astype(x.dtype)


def head_norm(y, g, eps):
    mu = jnp.mean(y, axis=-1, keepdims=True)
    var = jnp.mean(jnp.square(y - mu), axis=-1, keepdims=True)
    return (y - mu) * lax.rsqrt(var + eps) * g.astype(jnp.float32).reshape(y.shape[-2:])


def swiglu(h, w_gate, w_up, w_down):
    return (jax.nn.silu(h @ w_gate) * (h @ w_up)) @ w_down


def rel_bucket(dist):
    n = jnp.maximum(dist, 0)
    max_exact = REL_BUCKETS // 2
    large = max_exact + (jnp.log(jnp.maximum(n, max_exact).astype(jnp.float32) / max_exact)
                         / math.log(REL_MAX_DIST / max_exact) * (REL_BUCKETS - max_exact)).astype(jnp.int32)
    return jnp.where(n < max_exact, n, jnp.minimum(large, REL_BUCKETS - 1))


def compress_kv(kv, pos, w1, w2):
    b_, t_, kvh, hd = kv.shape
    n_cmp = t_ // CMP_STRIDE
    kp = jnp.pad(kv, ((0, 0), (0, CMP_STRIDE), (0, 0), (0, 0))).reshape(b_, n_cmp + 1, CMP_STRIDE, kvh, hd)
    blocks = jnp.concatenate([kp[:, :-1], kp[:, 1:]], axis=2) + pos[None, None, :, None, :]
    flat = blocks.transpose(0, 3, 1, 2, 4).reshape(b_, kvh, n_cmp, CMP_LEN * hd)
    return jax.nn.silu(flat @ w1) @ w2


def nsa_attention(q, k_cmp, v_cmp, k_slc, v_slc, k_win, v_win, gates, rel_bias):
    b_, kvh, g_, t_, hd = q.shape
    n_cmp = k_cmp.shape[2]
    n_blk = k_slc.shape[2]
    n_sel = min(N_SEL, n_blk)
    tab = rel_bias.astype(jnp.float32)
    tab_g = tab.reshape(kvh, g_, REL_BUCKETS)
    cmp_end = jnp.arange(n_cmp, dtype=jnp.int32) * CMP_STRIDE + (CMP_LEN - 1)
    blk_ids = jnp.arange(n_blk, dtype=jnp.int32)
    bi = jnp.arange(b_)[:, None, None, None]
    hi = jnp.arange(kvh)[None, :, None, None]
    hi6 = jnp.arange(kvh)[None, :, None, None, None, None]
    gi6 = jnp.arange(g_)[None, None, :, None, None, None]

    def block(qb):
        s0 = qb * Q_BLOCK
        t = s0 + jnp.arange(Q_BLOCK, dtype=jnp.int32)
        qq = lax.dynamic_slice_in_dim(q, s0, Q_BLOCK, axis=3)
        d_cmp = t[:, None] - cmp_end[None, :]
        ok_cmp = d_cmp >= 0
        s = (jnp.einsum('bhgqd,bhnd->bhgqn', qq, k_cmp).astype(jnp.float32)
             + tab[:, rel_bucket(d_cmp)].reshape(kvh, g_, Q_BLOCK, n_cmp))
        p_cmp = jax.nn.softmax(jnp.where(ok_cmp, s, MASK_NEG), axis=-1) * ok_cmp.any(-1)[:, None]
        o_cmp = jnp.einsum('bhgqn,bhnd->bhgqd', p_cmp.astype(v_cmp.dtype), v_cmp)
        imp = p_cmp.sum(2).reshape(b_, kvh, Q_BLOCK, n_blk, SEL_BLOCK // CMP_STRIDE)
        score = imp.sum(-1) + jnp.pad(imp[..., :-1, -1], ((0, 0), (0, 0), (0, 0), (1, 0)))
        cur = t // SEL_BLOCK
        forced = (blk_ids[None, :] == 0) | (blk_ids[None, :] == cur[:, None]) | (blk_ids[None, :] == cur[:, None] - 1)
        ok_blk = blk_ids[None, :] * SEL_BLOCK <= t[:, None]
        score = jnp.where(forced, -MASK_NEG, jnp.where(ok_blk, score, MASK_NEG))
        _, idx = lax.top_k(score, n_sel)
        kg = k_slc[bi, hi, idx]
        vg = v_slc[bi, hi, idx]
        pos = idx[..., None] * SEL_BLOCK + jnp.arange(SEL_BLOCK, dtype=jnp.int32)
        d_slc = (t[:, None, None] - pos)[:, :, None]
        s = (jnp.einsum('bhgqd,bhqnld->bhgqnl', qq, kg).astype(jnp.float32)
             + tab_g[hi6, gi6, rel_bucket(d_slc)])
        s = jnp.where(d_slc >= 0, s, MASK_NEG)
        p = jax.nn.softmax(s.reshape(b_, kvh, g_, Q_BLOCK, -1), axis=-1).reshape(s.shape)
        o_slc = jnp.einsum('bhgqnl,bhqnld->bhgqd', p.astype(vg.dtype), vg)
        kw = lax.dynamic_slice_in_dim(k_win, s0, WINDOW + Q_BLOCK, axis=2)
        vw = lax.dynamic_slice_in_dim(v_win, s0, WINDOW + Q_BLOCK, axis=2)
        kpos = s0 - WINDOW + jnp.arange(WINDOW + Q_BLOCK, dtype=jnp.int32)
        d_win = t[:, None] - kpos[None, :]
        ok_win = (d_win >= 0) & (d_win < WINDOW) & (kpos >= 0)[None, :]
        s = (jnp.einsum('bhgqd,bhkd->bhgqk', qq, kw).astype(jnp.float32)
             + tab[:, rel_bucket(d_win)].reshape(kvh, g_, Q_BLOCK, -1))
        p = jax.nn.softmax(jnp.where(ok_win, s, MASK_NEG), axis=-1)
        o_win = jnp.einsum('bhgqk,bhkd->bhgqd', p.astype(vw.dtype), vw)
        gq = lax.dynamic_slice_in_dim(gates, s0, Q_BLOCK, axis=3).astype(qq.dtype)
        o = gq[..., 0:1] * o_cmp + gq[..., 1:2] * o_slc + gq[..., 2:3] * o_win
        return o.transpose(0, 3, 1, 2, 4).reshape(b_, Q_BLOCK, kvh * g_ * hd)

    out = lax.map(block, jnp.arange(t_ // Q_BLOCK, dtype=jnp.int32))
    return out.transpose(1, 0, 2, 3).reshape(b_, t_, kvh * g_ * hd)


def nsa_group(z, rel_bias, cmp_pos, cmp_w1, cmp_w2):
    b_, t_, _ = z.shape
    q, kc, vc, ks, vs, kw, vw, gt = split_cols(z, [NSA_W] + [NSA_KV_W] * 6 + [3 * NSA_HEADS])
    q = q.reshape(b_, t_, NSA_KV_HEADS, NSA_GROUP, NSA_HEAD_DIM).transpose(0, 2, 3, 1, 4) * NSA_HEAD_DIM ** -0.5
    kv4 = lambda a: a.reshape(b_, t_, NSA_KV_HEADS, NSA_HEAD_DIM)
    k_cmp = compress_kv(kv4(kc), cmp_pos[0], cmp_w1[0], cmp_w2[0])
    v_cmp = compress_kv(kv4(vc), cmp_pos[1], cmp_w1[1], cmp_w2[1])
    blocks = lambda a: kv4(a).transpose(0, 2, 1, 3).reshape(b_, NSA_KV_HEADS, t_ // SEL_BLOCK, SEL_BLOCK, NSA_HEAD_DIM)
    band = lambda a: jnp.pad(kv4(a).transpose(0, 2, 1, 3), ((0, 0), (0, 0), (WINDOW, 0), (0, 0)))
    gates = jax.nn.sigmoid(gt.astype(jnp.float32)).reshape(b_, t_, NSA_KV_HEADS, NSA_GROUP, 3).transpose(0, 2, 3, 1, 4)
    return nsa_attention(q, k_cmp, v_cmp, blocks(ks), blocks(vs), band(kw), band(vw), gates, rel_bias)


def causal_depthwise_conv(x, w, b):
    k_ = w.shape[0]
    t_ = x.shape[1]
    xp = jnp.pad(x, ((0, 0), (k_ - 1, 0), (0, 0)))
    return sum(xp[:, j:j + t_] * w[j] for j in range(k_)) + b


def mlstm_chunkwise(q, k, v, i_pre, f_pre):
    b_, t_, h_, d_ = q.shape
    L = ML_CHUNK
    nc = t_ // L
    ch4 = lambda a: a.reshape(b_, nc, L, h_, d_).transpose(1, 0, 3, 2, 4)
    ch3 = lambda a: a.reshape(b_, nc, L, h_).transpose(1, 0, 3, 2)
    qc, kc, vc = ch4(q), ch4(k * d_ ** -0.5), ch4(v)
    ic = ch3(i_pre)
    bc = jnp.cumsum(ch3(jax.nn.log_sigmoid(f_pre)), axis=-1)
    causal = jnp.tril(jnp.ones((L, L), dtype=bool))

    def step(carry, inp):
        C, n, m = carry
        qx, kx, vx, ix, bx = inp
        g = bx[..., -1]
        log_d = jnp.where(causal, bx[..., :, None] - bx[..., None, :] + ix[..., None, :], -jnp.inf)
        m_inter = bx + m[..., None]
        m_t = jnp.maximum(log_d.max(-1), m_inter)
        s = jnp.einsum('bhtd,bhsd->bhts', qx, kx) * jnp.exp(log_d - m_t[..., None])
        w_inter = jnp.exp(m_inter - m_t)
        num = jnp.einsum('bhts,bhsd->bhtd', s, vx) + w_inter[..., None] * jnp.einsum('bhtd,bhde->bhte', qx, C)
        den = s.sum(-1) + w_inter * jnp.einsum('bhtd,bhd->bht', qx, n)
        h = num / jnp.maximum(jnp.abs(den), jnp.exp(-m_t))[..., None]
        log_w = g[..., None] - bx + ix
        m_new = jnp.maximum(g + m, log_w.max(-1))
        wk = jnp.exp(log_w - m_new[..., None])
        dec = jnp.exp(g + m - m_new)
        C = dec[..., None, None] * C + jnp.einsum('bhs,bhsd,bhse->bhde', wk, kx, vx)
        n = dec[..., None] * n + jnp.einsum('bhs,bhsd->bhd', wk, kx)
        return (C, n, m_new), h

    init = (jnp.zeros((b_, h_, d_, d_), jnp.float32), jnp.zeros((b_, h_, d_), jnp.float32),
            jnp.zeros((b_, h_), jnp.float32))
    _, hs = lax.scan(step, init, (qc, kc, vc, ic, bc))
    return hs.transpose(1, 0, 3, 2, 4).reshape(b_, t_, h_, d_)


def mlstm_group(z, conv_w, conv_b, gate_b, norm_g):
    b_, t_, _ = z.shape
    qk, v, o, ig, fg = split_cols(z, [2 * ML_W, ML_W, ML_W, ML_HEADS, ML_HEADS])
    qk = jax.nn.silu(causal_depthwise_conv(qk, conv_w, conv_b)).astype(jnp.float32)
    q, k = jnp.split(qk, 2, axis=-1)
    hd = lambda a: a.astype(jnp.float32).reshape(b_, t_, ML_HEADS, ML_HEAD_DIM)
    i_pre = ig.astype(jnp.float32) + gate_b[0].astype(jnp.float32)
    f_pre = fg.astype(jnp.float32) + gate_b[1].astype(jnp.float32)
    h = mlstm_chunkwise(hd(q), hd(k), hd(v), i_pre, f_pre)
    h = jax.nn.sigmoid(hd(o)) * h
    return head_norm(h, norm_g, NORM_EPS).reshape(b_, t_, ML_W).astype(z.dtype)


def rwkv7_scan(r, w, k, v, kk, a):
    b_, t_, h_, d_ = r.shape
    seq = tuple(jnp.moveaxis(u, 1, 0) for u in (r, w, k, v, -kk, kk * a))

    def step(S, inp):
        rt, wt, kt, vt, at, bt = inp
        sa = jnp.einsum('bhij,bhj->bhi', S, at)
        S = S * wt[:, :, None, :] + sa[..., None] * bt[:, :, None, :] + vt[..., None] * kt[:, :, None, :]
        return S, jnp.einsum('bhij,bhj->bhi', S, rt)

    _, y = lax.scan(step, jnp.zeros((b_, h_, d_, d_), jnp.float32), seq)
    return jnp.moveaxis(y, 0, 1)


def rwkv_group(z, mu, w0, w_up, a0, a_up, g_up, k_k, k_a, r_k, ln):
    b_, t_, _ = z.shape
    z_prev = jnp.pad(z, ((0, 0), (1, 0), (0, 0)))[:, :-1]
    z = z + mu * (z_prev - z)
    r, k, v, wd, ad, gd = [u.astype(jnp.float32) for u in
                           split_cols(z, [RW_W, RW_W, RW_W, RW_DECAY_RANK, RW_AICL_RANK, RW_GATE_RANK])]
    f32 = lambda p: p.astype(jnp.float32)
    w_log = -jax.nn.softplus(-(f32(w0) + jnp.tanh(wd) @ f32(w_up))) - 0.5
    decay = jnp.exp(-jnp.exp(w_log))
    a = jax.nn.sigmoid(f32(a0) + ad @ f32(a_up))
    g = jax.nn.sigmoid(gd) @ f32(g_up)
    hd = lambda u: u.reshape(b_, t_, RW_HEADS, RW_HEAD_DIM)
    kk = hd(k * f32(k_k))
    kk = kk / jnp.maximum(jnp.linalg.norm(kk, axis=-1, keepdims=True), 1e-12)
    k = k * (1.0 + (a - 1.0) * f32(k_a))
    r4, k4, v4 = hd(r), hd(k), hd(v)
    y = rwkv7_scan(r4, hd(decay), k4, v4, kk, hd(a))
    y = head_norm(y, ln[0], RW_LN_EPS) + f32(ln[1]).reshape(RW_HEADS, RW_HEAD_DIM)
    y = y + jnp.sum(r4 * k4 * f32(r_k), axis=-1, keepdims=True) * v4
    return (y.reshape(b_, t_, RW_W) * g).astype(z.dtype)


def setup_inputs(seed: int = 0) -> dict:
    key = jax.random.key(seed)
    ks = iter(jax.random.split(key, 48))
    nrm = lambda shape, scale: jax.random.normal(next(ks), shape, jnp.float32) * scale
    hdim = NSA_HEAD_DIM
    return {
        "x": nrm((BATCH, SEQ, D_MODEL), 1.0),
        "c": nrm((BATCH, D_MODEL), 1.0),
        "rel_bias": nrm((NSA_HEADS, REL_BUCKETS), 0.5),
        "final_norm": 1.0 + nrm((D_MODEL,), 0.02),
        "ada_w": nrm((DEPTH, D_MODEL, 9 * D_MODEL), 0.5 * D_MODEL ** -0.5),
        "ada_b": nrm((DEPTH, 9 * D_MODEL), 0.02),
        "norm_g": 1.0 + nrm((DEPTH, 3, D_MODEL), 0.02),
        "ffn_w_gate": nrm((DEPTH, 2, D_MODEL, D_FF), D_MODEL ** -0.5),
        "ffn_w_up": nrm((DEPTH, 2, D_MODEL, D_FF), D_MODEL ** -0.5),
        "ffn_w_down": nrm((DEPTH, 2, D_FF, D_MODEL), D_FF ** -0.5),
        "w_in": nrm((DEPTH, D_MODEL, N_IN), D_MODEL ** -0.5),
        "w_out": nrm((DEPTH, D_MIX, D_MODEL), D_MIX ** -0.5),
        "cmp_pos": nrm((DEPTH, 2, CMP_LEN, hdim), 0.1),
        "cmp_w1": nrm((DEPTH, 2, CMP_LEN * hdim, CMP_HIDDEN), (CMP_LEN * hdim) ** -0.5),
        "cmp_w2": nrm((DEPTH, 2, CMP_HIDDEN, hdim), CMP_HIDDEN ** -0.5),
        "ml_conv_w": nrm((DEPTH, ML_CONV, 2 * ML_W), ML_CONV ** -0.5),
        "ml_conv_b": nrm((DEPTH, 2 * ML_W), 0.02),
        "ml_gate_b": jnp.stack([nrm((DEPTH, ML_HEADS), 0.1),
                                jnp.linspace(3.0, 6.0, ML_HEADS)[None, :] + nrm((DEPTH, ML_HEADS), 0.1)], axis=1),
        "ml_norm": 1.0 + nrm((DEPTH, ML_W), 0.02),
        "rw_mu": jax.random.uniform(next(ks), (DEPTH, RW_COLS), jnp.float32),
        "rw_w0": nrm((DEPTH, RW_W), 0.5),
        "rw_w_up": nrm((DEPTH, RW_DECAY_RANK, RW_W), 0.1),
        "rw_a0": nrm((DEPTH, RW_W), 0.5),
        "rw_a_up": nrm((DEPTH, RW_AICL_RANK, RW_W), 0.1),
        "rw_g_up": nrm((DEPTH, RW_GATE_RANK, RW_W), RW_GATE_RANK ** -0.5),
        "rw_k_k": 0.85 + nrm((DEPTH, RW_W), 0.05),
        "rw_k_a": 1.0 + nrm((DEPTH, RW_W), 0.05),
        "rw_r_k": nrm((DEPTH, RW_HEADS, RW_HEAD_DIM), 0.1),
        "rw_ln": jnp.stack([1.0 + nrm((DEPTH, RW_W), 0.02), nrm((DEPTH, RW_W), 0.02)], axis=1),
    }


def reference(x, c, rel_bias, final_norm, ada_w, ada_b, norm_g, ffn_w_gate, ffn_w_up, ffn_w_down,
              w_in, w_out, cmp_pos, cmp_w1, cmp_w2, ml_conv_w, ml_conv_b, ml_gate_b, ml_norm,
              rw_mu, rw_w0, rw_w_up, rw_a0, rw_a_up, rw_g_up, rw_k_k, rw_k_a, rw_r_k, rw_ln):
    b_ = x.shape[0]
    cond = jax.nn.silu(c)
    for l in range(DEPTH):
        mod = (cond @ ada_w[l] + ada_b[l]).reshape(b_, 3, 3, D_MODEL)

        def adaln(u, i):
            return rmsnorm(u, norm_g[l, i]) * (1.0 + mod[:, i, 1, None]) + mod[:, i, 0, None]

        h = adaln(x, 0)
        x = x + 0.5 * mod[:, 0, 2, None] * swiglu(h, ffn_w_gate[l, 0], ffn_w_up[l, 0], ffn_w_down[l, 0])
        h = adaln(x, 1)
        z = h @ w_in[l]
        z_nsa, z_ml, z_rw = jnp.split(z, [NSA_COLS, NSA_COLS + ML_COLS], axis=-1)
        y_nsa = nsa_group(z_nsa, rel_bias, cmp_pos[l], cmp_w1[l], cmp_w2[l])
        y_ml = mlstm_group(z_ml, ml_conv_w[l], ml_conv_b[l], ml_gate_b[l], ml_norm[l])
        y_rw = rwkv_group(z_rw, rw_mu[l], rw_w0[l], rw_w_up[l], rw_a0[l], rw_a_up[l], rw_g_up[l],
                          rw_k_k[l], rw_k_a[l], rw_r_k[l], rw_ln[l])
        y = jnp.concatenate([y_nsa, y_ml, y_rw], axis=-1) @ w_out[l]
        x = x + mod[:, 1, 2, None] * y
        h = adaln(x, 2)
        x = x + 0.5 * mod[:, 2, 2, None] * swiglu(h, ffn_w_gate[l, 1], ffn_w_up[l, 1], ffn_w_down[l, 1])
    return rmsnorm(x, final_norm)
```

```python
import functools
import math

import numpy as np
import jax
import jax.numpy as jnp
from jax import lax
from jax.experimental import pallas as pl
from jax.experimental.pallas import tpu as pltpu

D_MODEL = 2048
DEPTH = 2
NSA_HEADS = 16
NSA_KV_HEADS = 4
NSA_HEAD_DIM = 64
NSA_GROUP = NSA_HEADS // NSA_KV_HEADS
CMP_LEN = 32
CMP_STRIDE = 16
CMP_HIDDEN = 128
SEL_BLOCK = 64
N_SEL = 16
WINDOW = 512
Q_BLOCK = 128
ML_HEADS = 4
ML_HEAD_DIM = 128
ML_CONV = 4
RW_HEADS = 8
RW_HEAD_DIM = 64
RW_DECAY_RANK = 64
RW_AICL_RANK = 64
RW_GATE_RANK = 128
D_FF = 5632
REL_BUCKETS = 32
REL_MAX_DIST = 128
NORM_EPS = 1e-6
RW_LN_EPS = 64e-5
MASK_NEG = -1e30

NSA_W = NSA_HEADS * NSA_HEAD_DIM
NSA_KV_W = NSA_KV_HEADS * NSA_HEAD_DIM
ML_W = ML_HEADS * ML_HEAD_DIM
RW_W = RW_HEADS * RW_HEAD_DIM
NSA_COLS = NSA_W + 6 * NSA_KV_W + 3 * NSA_HEADS
ML_COLS = 4 * ML_W + 2 * ML_HEADS
RW_COLS = 3 * RW_W + RW_DECAY_RANK + RW_AICL_RANK + RW_GATE_RANK

LANES = 128
SUBLANES = 8
VMEM_LIMIT = 56 * 1024 * 1024

F32 = jnp.float32
BF16 = jnp.bfloat16
HIGHEST = lax.Precision.HIGHEST

_Z_NSA_Q = 0
_Z_NSA_KV = 1024
_Z_NSA_GT = 2560
_Z_ML_GATE = 2688
_Z_RW_WA = 2816
_Z_RW_GD = 2944
_Z_ML = 3072
_Z_RW = 5120
Z_PAD = 6656


def _z_column_map():
    src = np.full((Z_PAD,), -1, np.int64)
    o_ml = NSA_COLS
    o_rw = NSA_COLS + ML_COLS
    src[_Z_NSA_Q:_Z_NSA_Q + NSA_W] = np.arange(NSA_W)
    src[_Z_NSA_KV:_Z_NSA_KV + 6 * NSA_KV_W] = NSA_W + np.arange(6 * NSA_KV_W)
    src[_Z_NSA_GT:_Z_NSA_GT + 3 * NSA_HEADS] = NSA_W + 6 * NSA_KV_W + np.arange(3 * NSA_HEADS)
    src[_Z_ML_GATE:_Z_ML_GATE + 2 * ML_HEADS] = o_ml + 4 * ML_W + np.arange(2 * ML_HEADS)
    src[_Z_RW_WA:_Z_RW_WA + 128] = o_rw + 3 * RW_W + np.arange(128)
    src[_Z_RW_GD:_Z_RW_GD + 128] = o_rw + 3 * RW_W + 128 + np.arange(128)
    src[_Z_ML:_Z_ML + 4 * ML_W] = o_ml + np.arange(4 * ML_W)
    src[_Z_RW:_Z_RW + 3 * RW_W] = o_rw + np.arange(3 * RW_W)
    return src


def _cparams(sem, vmem=VMEM_LIMIT):
    return pltpu.CompilerParams(dimension_semantics=sem, vmem_limit_bytes=vmem)


def _sigmoid(x):
    return 1.0 / (1.0 + jnp.exp(-x))


def _silu(x):
    return x * _sigmoid(x)


def _log_sigmoid(x):
    return -(jnp.maximum(-x, 0.0) + jnp.log(1.0 + jnp.exp(-jnp.abs(x))))


def _rms_mod(x, g, shift, scale):
    ms = jnp.mean(x * x, axis=-1, keepdims=True)
    y = x * lax.rsqrt(ms + NORM_EPS) * g
    return y * (1.0 + scale) + shift


def _mod_kernel(c_ref, w_ref, b_ref, o_ref):
    cond = _silu(c_ref[...]).astype(BF16)
    o_ref[...] = jnp.dot(cond, w_ref[...].astype(BF16), preferred_element_type=F32) + b_ref[...]


def _adaln_mod(c, ada_w, ada_b):
    b_, d = c.shape
    depth, _, n = ada_w.shape
    tn = 1024
    c8 = jnp.zeros((SUBLANES, d), F32).at[:b_].set(c)
    out = pl.pallas_call(
        _mod_kernel,
        out_shape=jax.ShapeDtypeStruct((depth, SUBLANES, n), F32),
        grid=(depth, n // tn),
        in_specs=[
            pl.BlockSpec((SUBLANES, d), lambda l, j: (0, 0)),
            pl.BlockSpec((None, d, tn), lambda l, j: (l, 0, j)),
            pl.BlockSpec((None, 1, tn), lambda l, j: (l, 0, j)),
        ],
        out_specs=pl.BlockSpec((None, SUBLANES, tn), lambda l, j: (l, 0, j)),
        compiler_params=_cparams(("parallel", "parallel")),
        name="adaln_mod",
    )(c8, ada_w, ada_b.reshape(depth, 1, n))
    return out[:, :b_].reshape(depth, b_, 9, d)


def _ffn_kernel(x_ref, mod_ref, g_ref, wg_ref, wu_ref, wd_ref, fin_ref, o_ref, h_scr, acc_scr, *, sub, n_f, final):
    f = pl.program_id(1)

    @pl.when(f == 0)
    def _():
        h = _rms_mod(x_ref[...], g_ref[sub:sub + 1, :], mod_ref[3 * sub:3 * sub + 1, :],
                     mod_ref[3 * sub + 1:3 * sub + 2, :])
        h_scr[...] = h.astype(BF16)
        acc_scr[...] = jnp.zeros_like(acc_scr)

    h = h_scr[...]
    gate = jnp.dot(h, wg_ref[...], preferred_element_type=F32)
    up = jnp.dot(h, wu_ref[...], preferred_element_type=F32)
    act = (_silu(gate) * up).astype(BF16)
    acc_scr[...] += jnp.dot(act, wd_ref[...], preferred_element_type=F32)

    @pl.when(f == n_f - 1)
    def _():
        y = x_ref[...] + 0.5 * mod_ref[3 * sub + 2:3 * sub + 3, :] * acc_scr[...]
        if final:
            ms = jnp.mean(y * y, axis=-1, keepdims=True)
            y = y * lax.rsqrt(ms + NORM_EPS) * fin_ref[...]
        o_ref[...] = y


def _ffn(x, mod_l, norm_g_l, wg, wu, wd, final_g, *, sub, rows_per_batch, final):
    m, d = x.shape
    dff = wg.shape[-1]
    tm, tf = 512, 512
    n_f = dff // tf
    bpb = rows_per_batch // tm
    return pl.pallas_call(
        functools.partial(_ffn_kernel, sub=sub, n_f=n_f, final=final),
        out_shape=jax.ShapeDtypeStruct((m, d), F32),
        grid=(m // tm, n_f),
        in_specs=[
            pl.BlockSpec((tm, d), lambda i, f: (i, 0)),
            pl.BlockSpec((None, 9, d), lambda i, f: (i // bpb, 0, 0)),
            pl.BlockSpec((3, d), lambda i, f: (0, 0)),
            pl.BlockSpec((d, tf), lambda i, f: (0, f)),
            pl.BlockSpec((d, tf), lambda i, f: (0, f)),
            pl.BlockSpec((tf, d), lambda i, f: (f, 0)),
            pl.BlockSpec((1, d), lambda i, f: (0, 0)),
        ],
        out_specs=pl.BlockSpec((tm, d), lambda i, f: (i, 0)),
        scratch_shapes=[pltpu.VMEM((tm, d), BF16), pltpu.VMEM((tm, d), F32)],
        compiler_params=_cparams(("parallel", "arbitrary")),
        name="ffn_swiglu",
    )(x, mod_l, norm_g_l, wg, wu, wd, final_g)


def _inproj_kernel(x_ref, mod_ref, g_ref, w_ref, o_ref, h_scr, *, sub):
    @pl.when(pl.program_id(1) == 0)
    def _():
        h = _rms_mod(x_ref[...], g_ref[sub:sub + 1, :], mod_ref[3 * sub:3 * sub + 1, :],
                     mod_ref[3 * sub + 1:3 * sub + 2, :])
        h_scr[...] = h.astype(BF16)

    o_ref[...] = jnp.dot(h_scr[...], w_ref[...], preferred_element_type=F32)


def _inproj(x, mod_l, norm_g_l, w_pad, *, sub, rows_per_batch):
    m, d = x.shape
    n = w_pad.shape[-1]
    tm, tn = 1024, 512
    bpb = rows_per_batch // tm
    return pl.pallas_call(
        functools.partial(_inproj_kernel, sub=sub),
        out_shape=jax.ShapeDtypeStruct((m, n), F32),
        grid=(m // tm, n // tn),
        in_specs=[
            pl.BlockSpec((tm, d), lambda i, j: (i, 0)),
            pl.BlockSpec((None, 9, d), lambda i, j: (i // bpb, 0, 0)),
            pl.BlockSpec((3, d), lambda i, j: (0, 0)),
            pl.BlockSpec((d, tn), lambda i, j: (0, j)),
        ],
        out_specs=pl.BlockSpec((tm, tn), lambda i, j: (i, j)),
        scratch_shapes=[pltpu.VMEM((tm, d), BF16)],
        compiler_params=_cparams(("parallel", "arbitrary")),
        name="in_proj",
    )(x, mod_l, norm_g_l, w_pad)


def _outproj_kernel(x_ref, mod_ref, yn_ref, ym_ref, yr_ref, w_ref, o_ref, *, sub):
    acc = jnp.dot(yn_ref[...].astype(BF16), w_ref[0:NSA_W, :], preferred_element_type=F32)
    acc += jnp.dot(ym_ref[...].astype(BF16), w_ref[NSA_W:NSA_W + ML_W, :], preferred_element_type=F32)
    acc += jnp.dot(yr_ref[...].astype(BF16), w_ref[NSA_W + ML_W:, :], preferred_element_type=F32)
    o_ref[...] = x_ref[...] + mod_ref[3 * sub + 2:3 * sub + 3, :] * acc


def _outproj(x, mod_l, y_nsa, y_ml, y_rw, w_out, *, sub, rows_per_batch):
    m, d = x.shape
    tm = 256
    bpb = rows_per_batch // tm
    return pl.pallas_call(
        functools.partial(_outproj_kernel, sub=sub),
        out_shape=jax.ShapeDtypeStruct((m, d), F32),
        grid=(m // tm,),
        in_specs=[
            pl.BlockSpec((tm, d), lambda i: (i, 0)),
            pl.BlockSpec((None, 9, d), lambda i: (i // bpb, 0, 0)),
            pl.BlockSpec((tm, NSA_W), lambda i: (i, 0)),
            pl.BlockSpec((tm, ML_W), lambda i: (i, 0)),
            pl.BlockSpec((tm, RW_W), lambda i: (i, 0)),
            pl.BlockSpec((d, d), lambda i: (0, 0)),
        ],
        out_specs=pl.BlockSpec((tm, d), lambda i: (i, 0)),
        compiler_params=_cparams(("parallel",)),
        name="out_proj",
    )(x, mod_l, y_nsa, y_ml, y_rw, w_out)


ML_CHUNK = 256


def _shift_rows(x, tail, s, row):
    xs = pltpu.roll(x, s, 0)
    ts = pltpu.roll(tail, s, 0)
    top = jnp.where(row[:SUBLANES] < s, ts, xs[:SUBLANES])
    return jnp.concatenate([top, xs[SUBLANES:]], axis=0)


def _mlstm_kernel(zq_ref, zk_ref, zv_ref, zo_ref, zg_ref, gt_ref, cw_ref, cb_ref, gbc_ref, gbr_ref, ng_ref,
                  o_ref, c_scr, n_scr, m_scr, tq_scr, tk_scr):
    L = zq_ref.shape[0]
    hd = ML_HEAD_DIM

    @pl.when(pl.program_id(1) == 0)
    def _():
        c_scr[...] = jnp.zeros_like(c_scr)
        n_scr[...] = jnp.zeros_like(n_scr)
        m_scr[...] = jnp.zeros_like(m_scr)
        tq_scr[...] = jnp.zeros_like(tq_scr)
        tk_scr[...] = jnp.zeros_like(tk_scr)

    row = lax.broadcasted_iota(jnp.int32, (L, 1), 0)

    def conv(x_ref, tail_scr, w, b):
        x = x_ref[...]
        tail = tail_scr[...]
        acc = x * w[ML_CONV - 1:ML_CONV, :] + b
        for s in range(1, ML_CONV):
            acc = acc + _shift_rows(x, tail, s, row) * w[ML_CONV - 1 - s:ML_CONV - s, :]
        tail_scr[...] = x[L - SUBLANES:, :]
        return _silu(acc)

    cw = cw_ref[...]
    cb = cb_ref[...]
    q_all = conv(zq_ref, tq_scr, cw[:, :ML_W], cb[:, :ML_W])
    k_all = conv(zk_ref, tk_scr, cw[:, ML_W:], cb[:, ML_W:]) * (hd ** -0.5)

    g_col = zg_ref[...] + gbc_ref[...]
    g_row = gt_ref[...] + gbr_ref[...]
    ti = lax.broadcasted_iota(jnp.int32, (L, L), 0)
    si = lax.broadcasted_iota(jnp.int32, (L, L), 1)
    causal = si <= ti
    tril = causal.astype(F32)
    triu = (ti <= si).astype(F32)
    bcum_col = jnp.dot(tril, _log_sigmoid(g_col), precision=HIGHEST, preferred_element_type=F32)
    bcum_row = jnp.dot(_log_sigmoid(g_row), triu, precision=HIGHEST, preferred_element_type=F32)

    for h in range(ML_HEADS):
        sl = slice(h * hd, (h + 1) * hd)
        q = q_all[:, sl]
        k = k_all[:, sl]
        v = zv_ref[:, sl]
        i_col = g_col[:, h:h + 1]
        b_col = bcum_col[:, ML_HEADS + h:ML_HEADS + h + 1]
        i_row = g_row[h:h + 1, :]
        b_row = bcum_row[ML_HEADS + h:ML_HEADS + h + 1, :]
        m_prev = m_scr[h][:, 0:1]
        c_prev = c_scr[h]
        n_prev = n_scr[h]

        log_d = jnp.where(causal, b_col + (i_row - b_row), MASK_NEG)
        m_inter = b_col + m_prev
        m_t = jnp.maximum(jnp.max(log_d, axis=-1, keepdims=True), m_inter)
        dmat = jnp.exp(log_d - m_t)
        qb = q.astype(BF16)
        kb = k.astype(BF16)
        vb = v.astype(BF16)
        s = lax.dot_general(qb, kb, (((1,), (1,)), ((), ())), preferred_element_type=F32) * dmat
        w_inter = jnp.exp(m_inter - m_t)
        num = jnp.dot(s.astype(BF16), vb, preferred_element_type=F32)
        num += w_inter * jnp.dot(qb, c_prev.astype(BF16), preferred_element_type=F32)
        den = jnp.sum(s, axis=-1, keepdims=True) + w_inter * jnp.sum(q * n_prev, axis=-1, keepdims=True)
        hh = num / jnp.maximum(jnp.abs(den), jnp.exp(-m_t))

        g_last = b_row[:, L - 1:L]
        lw_row = g_last - b_row + i_row
        m_new = jnp.maximum(g_last + m_prev, jnp.max(lw_row, axis=-1, keepdims=True))
        wk_col = jnp.exp(g_last - b_col + i_col - m_new)
        dec = jnp.exp(g_last + m_prev - m_new)
        kw = k * wk_col
        c_scr[h] = dec * c_prev + lax.dot_general(kw.astype(BF16), vb, (((0,), (0,)), ((), ())),
                                                  preferred_element_type=F32)
        n_scr[h] = dec * n_prev + jnp.sum(kw, axis=0, keepdims=True)
        m_scr[h] = jnp.broadcast_to(m_new, (1, LANES))

        hh = hh * _sigmoid(zo_ref[:, sl])
        mu = jnp.mean(hh, axis=-1, keepdims=True)
        var = jnp.mean(jnp.square(hh - mu), axis=-1, keepdims=True)
        o_ref[:, sl] = (hh - mu) * lax.rsqrt(var + NORM_EPS) * ng_ref[:, sl]


def _mlstm(z_pad, gates_t, conv_w, conv_b, gate_b, norm_g, *, batch, seq):
    L = min(ML_CHUNK, seq)
    nc = seq // L
    m = batch * seq
    gb = gate_b.reshape(2 * ML_HEADS)
    gb_col = jnp.zeros((1, LANES), F32).at[0, :2 * ML_HEADS].set(gb)
    gb_row = jnp.broadcast_to(gb[:, None], (2 * ML_HEADS, L))
    cb = lambda j: pl.BlockSpec((L, ML_W), lambda b, c: (b * nc + c, _Z_ML // ML_W + j))
    return pl.pallas_call(
        _mlstm_kernel,
        out_shape=jax.ShapeDtypeStruct((m, ML_W), F32),
        grid=(batch, nc),
        in_specs=[
            cb(0), cb(1), cb(2), cb(3),
            pl.BlockSpec((L, LANES), lambda b, c: (b * nc + c, _Z_ML_GATE // LANES)),
            pl.BlockSpec((2 * ML_HEADS, L), lambda b, c: (0, b * nc + c)),
            pl.BlockSpec((ML_CONV, 2 * ML_W), lambda b, c: (0, 0)),
            pl.BlockSpec((1, 2 * ML_W), lambda b, c: (0, 0)),
            pl.BlockSpec((1, LANES), lambda b, c: (0, 0)),
            pl.BlockSpec((2 * ML_HEADS, L), lambda b, c: (0, 0)),
            pl.BlockSpec((1, ML_W), lambda b, c: (0, 0)),
        ],
        out_specs=pl.BlockSpec((L, ML_W), lambda b, c: (b * nc + c, 0)),
        scratch_shapes=[
            pltpu.VMEM((ML_HEADS, ML_HEAD_DIM, ML_HEAD_DIM), F32),
            pltpu.VMEM((ML_HEADS, 1, ML_HEAD_DIM), F32),
            pltpu.VMEM((ML_HEADS, 1, LANES), F32),
            pltpu.VMEM((SUBLANES, ML_W), F32),
            pltpu.VMEM((SUBLANES, ML_W), F32),
        ],
        compiler_params=_cparams(("parallel", "arbitrary")),
        name="mlstm",
    )(z_pad, z_pad, z_pad, z_pad, z_pad, gates_t, conv_w, conv_b.reshape(1, -1), gb_col, gb_row,
      norm_g.reshape(1, -1))


RW_CHUNK = 64
RW_PAIRS = RW_HEADS // 2


def _dot_hi(a, b):
    return jnp.dot(a, b, precision=HIGHEST, preferred_element_type=F32)


def _dot_nt_hi(a, b):
    return lax.dot_general(a, b, (((1,), (1,)), ((), ())), precision=HIGHEST, preferred_element_type=F32)


def _dot_tn_hi(a, b):
    return lax.dot_general(a, b, (((0,), (0,)), ((), ())), precision=HIGHEST, preferred_element_type=F32)


def _rwkv_kernel(zr_ref, zk_ref, zv_ref, zwa_ref, zgd_ref, mur_ref, muk_ref, muv_ref, muwa_ref, mugd_ref,
                 w0_ref, wup_ref, a0_ref, aup_ref, gup_ref, kk_ref, ka_ref, rk_ref, ln_ref, ones_ref,
                 o_ref, s_scr, tr_scr, tk_scr, tv_scr, twa_scr, tgd_scr):
    C = zr_ref.shape[0]
    hd = RW_HEAD_DIM

    @pl.when(pl.program_id(1) == 0)
    def _():
        s_scr[...] = jnp.zeros_like(s_scr)
        tr_scr[...] = jnp.zeros_like(tr_scr)
        tk_scr[...] = jnp.zeros_like(tk_scr)
        tv_scr[...] = jnp.zeros_like(tv_scr)
        twa_scr[...] = jnp.zeros_like(twa_scr)
        tgd_scr[...] = jnp.zeros_like(tgd_scr)

    row = lax.broadcasted_iota(jnp.int32, (C, 1), 0)

    def shift(x_ref, t_scr, mu_ref):
        x = x_ref[...]
        xp = jnp.where(row == 0, t_scr[...], pltpu.roll(x, 1, 0))
        t_scr[...] = x[C - 1:C, :]
        return x + mu_ref[...] * (xp - x)

    zr = shift(zr_ref, tr_scr, mur_ref)
    zk = shift(zk_ref, tk_scr, muk_ref)
    zv = shift(zv_ref, tv_scr, muv_ref)
    zwa = shift(zwa_ref, twa_scr, muwa_ref)
    zgd = shift(zgd_ref, tgd_scr, mugd_ref)

    ones = ones_ref[...]
    u = -(w0_ref[...] + jnp.dot(jnp.tanh(zwa).astype(BF16), wup_ref[...], preferred_element_type=F32))
    softplus = jnp.maximum(u, 0.0) + jnp.log(1.0 + jnp.exp(-jnp.abs(u)))
    logw = -jnp.exp(-softplus - 0.5)
    a = _sigmoid(a0_ref[...] + jnp.dot(zwa.astype(BF16), aup_ref[...], preferred_element_type=F32))
    g = jnp.dot(_sigmoid(zgd).astype(BF16), gup_ref[...], preferred_element_type=F32)
    kk = zk * kk_ref[...]
    kk = kk / jnp.maximum(jnp.sqrt(_dot_hi(kk * kk, ones)), 1e-12)
    k2 = zk * (1.0 + (a - 1.0) * ka_ref[...])
    bv = kk * a

    tc = lax.broadcasted_iota(jnp.int32, (C, C), 0)
    sc = lax.broadcasted_iota(jnp.int32, (C, C), 1)
    lc = _dot_hi((sc <= tc).astype(F32), logw)
    l_last = lc[C - 1:C, :]
    e_neg = jnp.exp(-lc)
    e_end = jnp.exp(l_last - lc)
    at = -kk * jnp.exp(lc - logw)
    rt = zr * jnp.exp(lc)
    bt = bv * e_neg
    kt = k2 * e_neg
    bte = bv * e_end
    kte = k2 * e_end
    wc = jnp.exp(l_last)

    lane = lax.broadcasted_iota(jnp.int32, (1, LANES), 1)
    head_a = lane < hd
    t2 = lax.broadcasted_iota(jnp.int32, (2 * C, 2 * C), 0)
    s2 = lax.broadcasted_iota(jnp.int32, (2 * C, 2 * C), 1)
    same = (t2 < C) == (s2 < C)
    strict = same & (s2 < t2)
    incl = same & (s2 <= t2)
    eye = (s2 == t2).astype(F32)

    def stack_masked(x):
        return jnp.concatenate([jnp.where(head_a, x, 0.0), jnp.where(head_a, 0.0, x)], axis=0)

    def stack(x):
        return jnp.concatenate([x, x], axis=0)

    ys = []
    for p in range(RW_PAIRS):
        sl = slice(p * LANES, (p + 1) * LANES)
        at_s = stack_masked(at[:, sl])
        rt_s = stack_masked(rt[:, sl])
        b_s = stack(bt[:, sl])
        k_s = stack(kt[:, sl])
        v_s = stack_masked(zv[:, sl])
        a_ab = jnp.where(strict, _dot_nt_hi(at_s, b_s), 0.0)
        a_ak = jnp.where(strict, _dot_nt_hi(at_s, k_s), 0.0)
        b_rb = jnp.where(incl, _dot_nt_hi(rt_s, b_s), 0.0)
        b_rk = jnp.where(incl, _dot_nt_hi(rt_s, k_s), 0.0)
        x = a_ab
        tinv = eye + a_ab
        for _ in range(int(math.log2(C)) - 1):
            x = _dot_hi(x, x)
            tinv = tinv + _dot_hi(tinv, x)
        s0 = s_scr[p]
        uu = _dot_hi(tinv, _dot_nt_hi(at_s, s0) + _dot_hi(a_ak, v_s))
        yy = _dot_nt_hi(rt_s, s0) + _dot_hi(b_rb, uu) + _dot_hi(b_rk, v_s)
        ys.append(yy[:C] + yy[C:])
        s_scr[p] = (s0 * wc[:, sl] + _dot_tn_hi(uu, stack_masked(bte[:, sl]))
                    + _dot_tn_hi(v_s, stack_masked(kte[:, sl])))

    y = jnp.concatenate(ys, axis=1)
    inv_hd = 1.0 / hd
    mean = _dot_hi(y, ones) * inv_hd
    yc = y - mean
    var = _dot_hi(yc * yc, ones) * inv_hd
    yn = yc * lax.rsqrt(var + RW_LN_EPS) * ln_ref[0:1, :] + ln_ref[1:2, :]
    bonus = _dot_hi(zr * k2 * rk_ref[...], ones) * zv
    o_ref[...] = (yn + bonus) * g


def _rwkv(z_pad, mu, w0, w_up, a0, a_up, g_up, k_k, k_a, r_k, ln, *, batch, seq):
    C = RW_CHUNK
    nc = seq // C
    m = batch * seq
    w = RW_W
    row = lambda v: v.reshape(1, -1).astype(F32)
    wup_pad = jnp.zeros((LANES, w), F32).at[:RW_DECAY_RANK].set(w_up).astype(BF16)
    aup_pad = jnp.zeros((LANES, w), F32).at[RW_DECAY_RANK:].set(a_up).astype(BF16)
    head_of = np.arange(w) // RW_HEAD_DIM
    ones = jnp.asarray((head_of[:, None] == head_of[None, :]).astype(np.float32))
    zspec = lambda width, off: pl.BlockSpec((C, width), lambda b, c: (b * nc + c, off // width))
    full = lambda shape: pl.BlockSpec(shape, lambda b, c: tuple(0 for _ in shape))
    return pl.pallas_call(
        _rwkv_kernel,
        out_shape=jax.ShapeDtypeStruct((m, w), F32),
        grid=(batch, nc),
        in_specs=[
            zspec(w, _Z_RW), zspec(w, _Z_RW + w), zspec(w, _Z_RW + 2 * w),
            zspec(LANES, _Z_RW_WA), zspec(LANES, _Z_RW_GD),
            full((1, w)), full((1, w)), full((1, w)), full((1, LANES)), full((1, LANES)),
            full((1, w)), full((LANES, w)), full((1, w)), full((LANES, w)), full((LANES, w)),
            full((1, w)), full((1, w)), full((1, w)), full((2, w)), full((w, w)),
        ],
        out_specs=pl.BlockSpec((C, w), lambda b, c: (b * nc + c, 0)),
        scratch_shapes=[
            pltpu.VMEM((RW_PAIRS, LANES, LANES), F32),
            pltpu.VMEM((1, w), F32), pltpu.VMEM((1, w), F32), pltpu.VMEM((1, w), F32),
            pltpu.VMEM((1, LANES), F32), pltpu.VMEM((1, LANES), F32),
        ],
        compiler_params=_cparams(("parallel", "arbitrary")),
        name="rwkv7",
    )(z_pad, z_pad, z_pad, z_pad, z_pad,
      row(mu[:w]), row(mu[w:2 * w]), row(mu[2 * w:3 * w]), row(mu[3 * w:3 * w + LANES]), row(mu[3 * w + LANES:]),
      row(w0), wup_pad, row(a0), aup_pad, g_up.astype(BF16), row(k_k), row(k_a), row(r_k), ln.astype(F32), ones)


CMP_PER_Q = Q_BLOCK // CMP_STRIDE
CMP_PER_SEL = SEL_BLOCK // CMP_STRIDE
QG = NSA_GROUP * Q_BLOCK
FAR_BUCKET = REL_BUCKETS - 1
NSA_TILE = 2 * SEL_BLOCK


def _bucket_table(n_max):
    n = np.arange(n_max)
    max_exact = REL_BUCKETS // 2
    def large(dt):
        v = np.log(np.maximum(n, max_exact).astype(dt) / dt(max_exact)) / dt(math.log(REL_MAX_DIST / max_exact))
        return max_exact + (v * dt(REL_BUCKETS - max_exact)).astype(np.int32)
    l32, l64 = large(np.float32), large(np.float64)
    assert np.array_equal(np.minimum(l32, FAR_BUCKET), np.minimum(l64, FAR_BUCKET))
    return np.where(n < max_exact, n, np.minimum(l64, FAR_BUCKET)).astype(np.int32)


_FAR_DIST = int(np.argmax(_bucket_table(4096) == FAR_BUCKET))
assert np.all(_bucket_table(4096)[_FAR_DIST:] == FAR_BUCKET) and _FAR_DIST <= Q_BLOCK - CMP_STRIDE + 1


def _rel_bias_tile(rel_bias, dist, valid):
    table = _bucket_table(int(dist.max()) + 2)
    idx = table[np.clip(dist, 0, None)]
    tile = rel_bias.astype(F32)[:, idx]
    tile = jnp.where(jnp.asarray(valid)[None], tile, MASK_NEG)
    rows = dist.shape[0]
    tile = tile.reshape(NSA_KV_HEADS, NSA_GROUP, rows, Q_BLOCK).transpose(0, 2, 1, 3)
    return tile.reshape(NSA_KV_HEADS, rows, QG)


def _compress_kernel(r_ref, pos_ref, w1_ref, w2_ref, o_ref):
    nc = r_ref.shape[0]
    half = CMP_STRIDE * NSA_HEAD_DIM
    r = r_ref[...]
    w1 = w1_ref[...].astype(BF16)
    p_top = pos_ref[0:1, :]
    p_bot = pos_ref[1:2, :]
    g1 = jnp.dot((r + p_top).astype(BF16), w1[:half], preferred_element_type=F32)
    g2 = jnp.dot((r + p_bot).astype(BF16), w1[half:], preferred_element_type=F32)
    tail = jnp.dot(jnp.broadcast_to(p_bot, (SUBLANES, half)).astype(BF16), w1[half:],
                   preferred_element_type=F32)[0:1]
    row = lax.broadcasted_iota(jnp.int32, (nc, 1), 0)
    g2_next = jnp.where(row == nc - 1, tail, pltpu.roll(g2, nc - 1, 0))
    hid = _silu(g1 + g2_next)
    o_ref[...] = jnp.dot(hid.astype(BF16), w2_ref[...].astype(BF16), preferred_element_type=F32)


def _compress(r_kv, cmp_pos, cmp_w1, cmp_w2):
    two, bh, nc, width = r_kv.shape
    pos = cmp_pos.reshape(2, 2, width)
    return pl.pallas_call(
        _compress_kernel,
        out_shape=jax.ShapeDtypeStruct((2, bh, nc, NSA_HEAD_DIM), F32),
        grid=(2, bh),
        in_specs=[
            pl.BlockSpec((None, None, nc, width), lambda s, i: (s, i, 0, 0)),
            pl.BlockSpec((None, 2, width), lambda s, i: (s, 0, 0)),
            pl.BlockSpec((None, 2 * width, CMP_HIDDEN), lambda s, i: (s, 0, 0)),
            pl.BlockSpec((None, CMP_HIDDEN, NSA_HEAD_DIM), lambda s, i: (s, 0, 0)),
        ],
        out_specs=pl.BlockSpec((None, None, nc, NSA_HEAD_DIM), lambda s, i: (s, i, 0, 0)),
        compiler_params=_cparams(("parallel", "parallel")),
        name="nsa_compress",
    )(r_kv, pos, cmp_w1, cmp_w2)


def _nsa_cmp_kernel(qt_ref, kc_ref, vct_ref, far_ref, delta_ref, ocmp_ref, sel_ref, flag_ref, s_scr, imp_scr,
                    *, n_sel):
    qb = pl.program_id(2)
    nc = kc_ref.shape[0]
    nb = nc // CMP_PER_SEL
    pad = SUBLANES

    s = jnp.dot(kc_ref[...].astype(BF16), qt_ref[...], preferred_element_type=F32) + far_ref[...]
    s_scr[0:pad, :] = jnp.zeros((pad, QG), F32)
    s_scr[pad:, :] = s
    win = pl.ds(pl.multiple_of(CMP_PER_Q * qb, SUBLANES), 2 * CMP_PER_Q)
    s_scr[win, :] = s_scr[win, :] + delta_ref[...]
    s = s_scr[pad:, :]

    n_idx = lax.broadcasted_iota(jnp.int32, (nc, 1), 0)
    lane = lax.broadcasted_iota(jnp.int32, (1, QG), 1)
    t_q = qb * Q_BLOCK + (lane & (Q_BLOCK - 1))
    valid = n_idx * CMP_STRIDE + (CMP_LEN - 1) <= t_q
    m = jnp.max(jnp.where(valid, s, MASK_NEG), axis=0, keepdims=True)
    p = jnp.where(valid, jnp.exp(s - m), 0.0)
    l = jnp.sum(p, axis=0, keepdims=True)
    pn = p * jnp.where(l > 0.0, 1.0 / l, 0.0)
    ocmp_ref[...] = jnp.dot(vct_ref[...].astype(BF16), pn.astype(BF16), preferred_element_type=F32)

    imp = pn[:, 0:Q_BLOCK]
    for g in range(1, NSA_GROUP):
        imp = imp + pn[:, g * Q_BLOCK:(g + 1) * Q_BLOCK]
    imp_scr[...] = imp
    parts = [imp_scr[pl.ds(c, nb, stride=CMP_PER_SEL), :] for c in range(CMP_PER_SEL)]
    score = parts[0]
    for c in range(1, CMP_PER_SEL):
        score = score + parts[c]
    blk = lax.broadcasted_iota(jnp.int32, (nb, 1), 0)
    score = score + jnp.where(blk == 0, 0.0, pltpu.roll(parts[-1], 1, 0))

    t = qb * Q_BLOCK + lax.broadcasted_iota(jnp.int32, (1, Q_BLOCK), 1)
    cur = t // SEL_BLOCK
    forced = (blk == 0) | (blk == cur) | (blk == cur - 1)
    ok = blk * SEL_BLOCK <= t
    score = jnp.where(forced, -MASK_NEG, jnp.where(ok, score, MASK_NEG))
    sel = jnp.zeros((nb, Q_BLOCK), F32)
    for _ in range(n_sel):
        top = jnp.max(score, axis=0, keepdims=True)
        first = jnp.min(jnp.where(score == top, blk, nb), axis=0, keepdims=True)
        hit = blk == first
        sel = jnp.where(hit, 1.0, sel)
        score = jnp.where(hit, -jnp.inf, score)
    sel = jnp.where(ok, sel, 0.0)
    sel_ref[...] = sel
    cnt = lax.dot_general(jnp.ones((SUBLANES, Q_BLOCK), F32), sel, (((1,), (1,)), ((), ())),
                          preferred_element_type=F32)
    flag_ref[...] = (cnt[0:1, :] > 0.0).astype(jnp.int32)


def _nsa_attn_kernel(flag_ref, qt_ref, sel_ref, ocmp_ref, gate_ref, ks_ref, vst_ref, kw_ref, vwt_ref,
                     far_ref, bdiag_ref, bwin_ref, o_ref, m_scr, l_scr, acc_scr):
    qb = pl.program_id(2)
    qt = qt_ref[...]

    def reset():
        m_scr[...] = jnp.full_like(m_scr, MASK_NEG)
        l_scr[...] = jnp.zeros_like(l_scr)
        acc_scr[...] = jnp.zeros_like(acc_scr)

    def tile_start(tile):
        return pl.multiple_of(tile * NSA_TILE, NSA_TILE)

    def sel_mask(tile):
        rows = []
        for j in range(NSA_TILE // SEL_BLOCK):
            r = sel_ref[pl.ds(tile * (NSA_TILE // SEL_BLOCK) + j, 1), :]
            r = jnp.concatenate([r] * NSA_GROUP, axis=1)
            rows.append(jnp.broadcast_to(r, (SEL_BLOCK, QG)))
        return jnp.concatenate(rows, axis=0) > 0.5

    def update(k_ref, vt_ref, tile, bias, mask):
        k_rows = k_ref[pl.ds(tile_start(tile), NSA_TILE), :]
        vt = vt_ref[:, pl.ds(tile_start(tile), NSA_TILE)]
        s = jnp.dot(k_rows, qt, preferred_element_type=F32) + bias
        if mask is not None:
            s = jnp.where(mask, s, MASK_NEG)
        m_old = m_scr[...]
        m_new = jnp.maximum(m_old, jnp.max(s, axis=0, keepdims=True))
        alpha = jnp.exp(m_old - m_new)
        p = jnp.exp(s - m_new)
        l_scr[...] = alpha * l_scr[...] + jnp.sum(p, axis=0, keepdims=True)
        acc_scr[...] = alpha * acc_scr[...] + jnp.dot(vt, p.astype(BF16), preferred_element_type=F32)
        m_scr[...] = m_new

    def finish():
        return acc_scr[...] * (1.0 / l_scr[...])

    reset()
    per_tile = NSA_TILE // SEL_BLOCK

    def far_body(tile, carry):
        active = flag_ref[0, per_tile * tile] + flag_ref[0, per_tile * tile + 1]

        @pl.when(active > 0)
        def _():
            update(ks_ref, vst_ref, tile, far_ref[...], sel_mask(tile))
        return carry

    lax.fori_loop(0, jnp.maximum(qb - 1, 0), far_body, 0)

    @pl.when(qb > 0)
    def _():
        update(ks_ref, vst_ref, qb - 1, bdiag_ref[0:NSA_TILE, :], sel_mask(qb - 1))

    update(ks_ref, vst_ref, qb, bdiag_ref[NSA_TILE:, :], sel_mask(qb))
    o_slc = finish()

    reset()
    n_win = WINDOW // NSA_TILE + 1
    for j in range(n_win):
        tile = qb - (n_win - 1) + j
        if j == n_win - 1:
            update(kw_ref, vwt_ref, tile, bwin_ref[j * NSA_TILE:(j + 1) * NSA_TILE, :], None)
        else:
            @pl.when(tile >= 0)
            def _():
                update(kw_ref, vwt_ref, tile, bwin_ref[j * NSA_TILE:(j + 1) * NSA_TILE, :], None)
    o_win = finish()

    sig = _sigmoid(gate_ref[...])

    def gate_row(branch):
        return jnp.concatenate([sig[g * 3 + branch:g * 3 + branch + 1, :] for g in range(NSA_GROUP)], axis=1)

    o_t = gate_row(0) * ocmp_ref[...] + gate_row(1) * o_slc + gate_row(2) * o_win
    stacked = jnp.concatenate([o_t[:, g * Q_BLOCK:(g + 1) * Q_BLOCK] for g in range(NSA_GROUP)], axis=0)
    o_ref[...] = stacked.T


def _nsa(z_pad, rel_bias, cmp_pos, cmp_w1, cmp_w2, *, batch, seq):
    kvh, grp, hd = NSA_KV_HEADS, NSA_GROUP, NSA_HEAD_DIM
    bh = batch * kvh
    nqb = seq // Q_BLOCK
    nc = seq // CMP_STRIDE
    nb = seq // SEL_BLOCK
    n_sel = min(N_SEL, nb)
    m = batch * seq

    zq = z_pad[:, _Z_NSA_Q:_Z_NSA_Q + NSA_W].reshape(batch, nqb, Q_BLOCK, kvh, grp, hd)
    qt = (zq * hd ** -0.5).astype(BF16).transpose(0, 3, 1, 5, 4, 2).reshape(bh, nqb, hd, QG)
    kv = z_pad[:, _Z_NSA_KV:_Z_NSA_KV + 6 * NSA_KV_W].reshape(batch, seq, 6, kvh, hd)
    kv = kv.transpose(2, 0, 3, 1, 4).reshape(6, bh, seq, hd)
    r_kv = kv[0:2].reshape(2, bh, nc, CMP_STRIDE * hd)
    ks = kv[2].astype(BF16)
    vst = kv[3].astype(BF16).transpose(0, 2, 1)
    kw = kv[4].astype(BF16)
    vwt = kv[5].astype(BF16).transpose(0, 2, 1)
    gates_t = z_pad[:, _Z_NSA_GT:_Z_NSA_GT + 3 * NSA_HEADS].reshape(batch, seq, kvh, grp * 3)
    gates_t = gates_t.transpose(0, 2, 3, 1).reshape(bh, grp * 3, seq)

    qi = np.arange(Q_BLOCK)[None, :]
    far = _rel_bias_tile(rel_bias, np.full((1, Q_BLOCK), _FAR_DIST), np.ones((1, Q_BLOCK), bool))
    mrow = np.arange(-CMP_PER_Q, CMP_PER_Q)[:, None]
    d_cmp = qi - CMP_STRIDE * mrow - (CMP_LEN - 1)
    delta = _rel_bias_tile(rel_bias, d_cmp, d_cmp >= 0)
    delta = jnp.where(jnp.asarray(np.tile(d_cmp >= 0, (1, grp)))[None], delta - far, 0.0)
    d_diag = qi + NSA_TILE - np.arange(2 * NSA_TILE)[:, None]
    bdiag = _rel_bias_tile(rel_bias, d_diag, d_diag >= 0)
    d_win = qi + WINDOW - np.arange(WINDOW + NSA_TILE)[:, None]
    bwin = _rel_bias_tile(rel_bias, d_win, (d_win >= 0) & (d_win < WINDOW))

    cmp_kv = _compress(r_kv, cmp_pos, cmp_w1, cmp_w2)
    k_cmp = cmp_kv[0]
    v_cmp_t = cmp_kv[1].transpose(0, 2, 1)

    g3 = (batch, kvh, nqb)
    bhq = lambda b, h, q: (b * kvh + h, q, 0, 0)
    bh0 = lambda b, h, q: (b * kvh + h, 0, 0)
    per_h = lambda b, h, q: (h, 0, 0)
    o_cmp, sel, flags = pl.pallas_call(
        functools.partial(_nsa_cmp_kernel, n_sel=n_sel),
        out_shape=(jax.ShapeDtypeStruct((bh, nqb, hd, QG), F32),
                   jax.ShapeDtypeStruct((bh, nqb, nb, Q_BLOCK), F32),
                   jax.ShapeDtypeStruct((bh, nqb, 1, nb), jnp.int32)),
        grid=g3,
        in_specs=[
            pl.BlockSpec((None, None, hd, QG), bhq),
            pl.BlockSpec((None, nc, hd), bh0),
            pl.BlockSpec((None, hd, nc), bh0),
            pl.BlockSpec((None, 1, QG), per_h),
            pl.BlockSpec((None, 2 * CMP_PER_Q, QG), per_h),
        ],
        out_specs=(pl.BlockSpec((None, None, hd, QG), bhq),
                   pl.BlockSpec((None, None, nb, Q_BLOCK), bhq),
                   pl.BlockSpec((None, None, 1, nb), bhq)),
        scratch_shapes=[pltpu.VMEM((nc + SUBLANES, QG), F32), pltpu.VMEM((nc, Q_BLOCK), F32)],
        compiler_params=_cparams(("parallel", "parallel", "arbitrary")),
        name="nsa_cmp_select",
    )(qt, k_cmp, v_cmp_t, far, delta)

    return pl.pallas_call(
        _nsa_attn_kernel,
        out_shape=jax.ShapeDtypeStruct((m, NSA_W), F32),
        grid=g3,
        in_specs=[
            pl.BlockSpec((None, None, 1, nb), bhq, memory_space=pltpu.SMEM),
            pl.BlockSpec((None, None, hd, QG), bhq),
            pl.BlockSpec((None, None, nb, Q_BLOCK), bhq),
            pl.BlockSpec((None, None, hd, QG), bhq),
            pl.BlockSpec((None, grp * 3, Q_BLOCK), lambda b, h, q: (b * kvh + h, 0, q)),
            pl.BlockSpec((None, seq, hd), bh0),
            pl.BlockSpec((None, hd, seq), bh0),
            pl.BlockSpec((None, seq, hd), bh0),
            pl.BlockSpec((None, hd, seq), bh0),
            pl.BlockSpec((None, 1, QG), per_h),
            pl.BlockSpec((None, 2 * NSA_TILE, QG), per_h),
            pl.BlockSpec((None, WINDOW + NSA_TILE, QG), per_h),
        ],
        out_specs=pl.BlockSpec((Q_BLOCK, grp * hd), lambda b, h, q: (b * nqb + q, h)),
        scratch_shapes=[pltpu.VMEM((1, QG), F32), pltpu.VMEM((1, QG), F32), pltpu.VMEM((hd, QG), F32)],
        compiler_params=_cparams(("parallel", "parallel", "arbitrary")),
        name="nsa_attention",
    )(flags, qt, sel, o_cmp, gates_t, ks, vst, kw, vwt, far, bdiag, bwin)


def kernel(x, c, rel_bias, final_norm, ada_w, ada_b, norm_g, ffn_w_gate, ffn_w_up, ffn_w_down, w_in, w_out,
           cmp_pos, cmp_w1, cmp_w2, ml_conv_w, ml_conv_b, ml_gate_b, ml_norm, rw_mu, rw_w0, rw_w_up, rw_a0,
           rw_a_up, rw_g_up, rw_k_k, rw_k_a, rw_r_k, rw_ln):
    b_, t_, d = x.shape
    m = b_ * t_
    mod = _adaln_mod(c, ada_w, ada_b)
    src = _z_column_map()
    col_ok = jnp.asarray(src >= 0)
    col_src = np.maximum(src, 0)
    fin = final_norm.reshape(1, d)
    xf = x.reshape(m, d)
    for l in range(DEPTH):
        wg = ffn_w_gate[l].astype(BF16)
        wu = ffn_w_up[l].astype(BF16)
        wd = ffn_w_down[l].astype(BF16)
        w_pad = jnp.where(col_ok[None, :], w_in[l][:, col_src], 0.0).astype(BF16)
        xf = _ffn(xf, mod[l], norm_g[l], wg[0], wu[0], wd[0], fin, sub=0, rows_per_batch=t_, final=False)
        z_pad = _inproj(xf, mod[l], norm_g[l], w_pad, sub=1, rows_per_batch=t_)
        gates_t = z_pad[:, _Z_ML_GATE:_Z_ML_GATE + 2 * ML_HEADS].T
        y_ml = _mlstm(z_pad, gates_t, ml_conv_w[l], ml_conv_b[l], ml_gate_b[l], ml_norm[l], batch=b_, seq=t_)
        y_nsa = _nsa(z_pad, rel_bias, cmp_pos[l], cmp_w1[l], cmp_w2[l], batch=b_, seq=t_)
        y_rw = _rwkv(z_pad, rw_mu[l], rw_w0[l], rw_w_up[l], rw_a0[l], rw_a_up[l], rw_g_up[l], rw_k_k[l],
                     rw_k_a[l], rw_r_k[l], rw_ln[l], batch=b_, seq=t_)
        xf = _outproj(xf, mod[l], y_nsa, y_ml, y_rw, w_out[l].astype(BF16), sub=1, rows_per_batch=t_)
        xf = _ffn(xf, mod[l], norm_g[l], wg[1], wu[1], wd[1], fin, sub=2, rows_per_batch=t_,
                  final=(l == DEPTH - 1))
    return xf.reshape(b_, t_, d)
```

```python
import functools
import math

import numpy as np
import jax
import jax.numpy as jnp
from jax import lax
from jax.experimental import pallas as pl
from jax.experimental.pallas import tpu as pltpu

D_MODEL = 2048
DEPTH = 2
NSA_HEADS = 16
NSA_KV_HEADS = 4
NSA_HEAD_DIM = 64
NSA_GROUP = NSA_HEADS // NSA_KV_HEADS
CMP_LEN = 32
CMP_STRIDE = 16
CMP_HIDDEN = 128
SEL_BLOCK = 64
N_SEL = 16
WINDOW = 512
Q_BLOCK = 128
ML_HEADS = 4
ML_HEAD_DIM = 128
ML_CONV = 4
RW_HEADS = 8
RW_HEAD_DIM = 64
RW_DECAY_RANK = 64
RW_AICL_RANK = 64
RW_GATE_RANK = 128
D_FF = 5632
REL_BUCKETS = 32
REL_MAX_DIST = 128
NORM_EPS = 1e-6
RW_LN_EPS = 64e-5
MASK_NEG = -1e30

NSA_W = NSA_HEADS * NSA_HEAD_DIM
NSA_KV_W = NSA_KV_HEADS * NSA_HEAD_DIM
ML_W = ML_HEADS * ML_HEAD_DIM
RW_W = RW_HEADS * RW_HEAD_DIM
NSA_COLS = NSA_W + 6 * NSA_KV_W + 3 * NSA_HEADS
ML_COLS = 4 * ML_W + 2 * ML_HEADS
RW_COLS = 3 * RW_W + RW_DECAY_RANK + RW_AICL_RANK + RW_GATE_RANK

LANES = 128
SUBLANES = 8
VMEM_LIMIT = 56 * 1024 * 1024

F32 = jnp.float32
BF16 = jnp.bfloat16
HIGHEST = lax.Precision.HIGHEST

_Z_NSA_Q = 0
_Z_NSA_KV = 1024
_Z_NSA_GT = 2560
_Z_ML_GATE = 2688
_Z_RW_WA = 2816
_Z_RW_GD = 2944
_Z_ML = 3072
_Z_RW = 5120
Z_PAD = 6656


def _z_column_map():
    src = np.full((Z_PAD,), -1, np.int64)
    o_ml = NSA_COLS
    o_rw = NSA_COLS + ML_COLS
    src[_Z_NSA_Q:_Z_NSA_Q + NSA_W] = np.arange(NSA_W)
    src[_Z_NSA_KV:_Z_NSA_KV + 6 * NSA_KV_W] = NSA_W + np.arange(6 * NSA_KV_W)
    src[_Z_NSA_GT:_Z_NSA_GT + 3 * NSA_HEADS] = NSA_W + 6 * NSA_KV_W + np.arange(3 * NSA_HEADS)
    src[_Z_ML_GATE:_Z_ML_GATE + 2 * ML_HEADS] = o_ml + 4 * ML_W + np.arange(2 * ML_HEADS)
    src[_Z_RW_WA:_Z_RW_WA + 128] = o_rw + 3 * RW_W + np.arange(128)
    src[_Z_RW_GD:_Z_RW_GD + 128] = o_rw + 3 * RW_W + 128 + np.arange(128)
    src[_Z_ML:_Z_ML + 4 * ML_W] = o_ml + np.arange(4 * ML_W)
    src[_Z_RW:_Z_RW + 3 * RW_W] = o_rw + np.arange(3 * RW_W)
    return src


def _cparams(sem, vmem=VMEM_LIMIT):
    return pltpu.CompilerParams(dimension_semantics=sem, vmem_limit_bytes=vmem)


def _sigmoid(x):
    return 1.0 / (1.0 + jnp.exp(-x))


def _silu(x):
    return x * _sigmoid(x)


def _log_sigmoid(x):
    return -(jnp.maximum(-x, 0.0) + jnp.log(1.0 + jnp.exp(-jnp.abs(x))))


def _rms_mod(x, g, shift, scale):
    ms = jnp.mean(x * x, axis=-1, keepdims=True)
    y = x * lax.rsqrt(ms + NORM_EPS) * g
    return y * (1.0 + scale) + shift


def _mod_kernel(c_ref, w_ref, b_ref, o_ref):
    cond = _silu(c_ref[...]).astype(BF16)
    o_ref[...] = jnp.dot(cond, w_ref[...].astype(BF16), preferred_element_type=F32) + b_ref[...]


def _adaln_mod(c, ada_w, ada_b):
    b_, d = c.shape
    depth, _, n = ada_w.shape
    tn = 1024
    c8 = jnp.zeros((SUBLANES, d), F32).at[:b_].set(c)
    out = pl.pallas_call(
        _mod_kernel,
        out_shape=jax.ShapeDtypeStruct((depth, SUBLANES, n), F32),
        grid=(depth, n // tn),
        in_specs=[
            pl.BlockSpec((SUBLANES, d), lambda l, j: (0, 0)),
            pl.BlockSpec((None, d, tn), lambda l, j: (l, 0, j)),
            pl.BlockSpec((None, 1, tn), lambda l, j: (l, 0, j)),
        ],
        out_specs=pl.BlockSpec((None, SUBLANES, tn), lambda l, j: (l, 0, j)),
        compiler_params=_cparams(("parallel", "parallel")),
        name="adaln_mod",
    )(c8, ada_w, ada_b.reshape(depth, 1, n))
    return out[:, :b_].reshape(depth, b_, 9, d)


def _ffn_kernel(x_ref, mod_ref, g_ref, wg_ref, wu_ref, wd_ref, fin_ref, o_ref, h_scr, acc_scr, *, sub, n_f, final):
    f = pl.program_id(1)

    @pl.when(f == 0)
    def _():
        h = _rms_mod(x_ref[...], g_ref[sub:sub + 1, :], mod_ref[3 * sub:3 * sub + 1, :],
                     mod_ref[3 * sub + 1:3 * sub + 2, :])
        h_scr[...] = h.astype(BF16)
        acc_scr[...] = jnp.zeros_like(acc_scr)

    h = h_scr[...]
    gate = jnp.dot(h, wg_ref[...], preferred_element_type=F32)
    up = jnp.dot(h, wu_ref[...], preferred_element_type=F32)
    act = (_silu(gate) * up).astype(BF16)
    acc_scr[...] += jnp.dot(act, wd_ref[...], preferred_element_type=F32)

    @pl.when(f == n_f - 1)
    def _():
        y = x_ref[...] + 0.5 * mod_ref[3 * sub + 2:3 * sub + 3, :] * acc_scr[...]
        if final:
            ms = jnp.mean(y * y, axis=-1, keepdims=True)
            y = y * lax.rsqrt(ms + NORM_EPS) * fin_ref[...]
        o_ref[...] = y


def _ffn(x, mod_l, norm_g_l, wg, wu, wd, final_g, *, sub, rows_per_batch, final):
    m, d = x.shape
    dff = wg.shape[-1]
    tm, tf = 512, 512
    n_f = dff // tf
    bpb = rows_per_batch // tm
    return pl.pallas_call(
        functools.partial(_ffn_kernel, sub=sub, n_f=n_f, final=final),
        out_shape=jax.ShapeDtypeStruct((m, d), F32),
        grid=(m // tm, n_f),
        in_specs=[
            pl.BlockSpec((tm, d), lambda i, f: (i, 0)),
            pl.BlockSpec((None, 9, d), lambda i, f: (i // bpb, 0, 0)),
            pl.BlockSpec((3, d), lambda i, f: (0, 0)),
            pl.BlockSpec((d, tf), lambda i, f: (0, f)),
            pl.BlockSpec((d, tf), lambda i, f: (0, f)),
            pl.BlockSpec((tf, d), lambda i, f: (f, 0)),
            pl.BlockSpec((1, d), lambda i, f: (0, 0)),
        ],
        out_specs=pl.BlockSpec((tm, d), lambda i, f: (i, 0)),
        scratch_shapes=[pltpu.VMEM((tm, d), BF16), pltpu.VMEM((tm, d), F32)],
        compiler_params=_cparams(("parallel", "arbitrary")),
        name="ffn_swiglu",
    )(x, mod_l, norm_g_l, wg, wu, wd, final_g)


def _inproj_kernel(x_ref, mod_ref, g_ref, w_ref, o_ref, h_scr, *, sub):
    @pl.when(pl.program_id(1) == 0)
    def _():
        h = _rms_mod(x_ref[...], g_ref[sub:sub + 1, :], mod_ref[3 * sub:3 * sub + 1, :],
                     mod_ref[3 * sub + 1:3 * sub + 2, :])
        h_scr[...] = h.astype(BF16)

    o_ref[...] = jnp.dot(h_scr[...], w_ref[...], preferred_element_type=F32)


def _inproj(x, mod_l, norm_g_l, w_pad, *, sub, rows_per_batch):
    m, d = x.shape
    n = w_pad.shape[-1]
    tm, tn = 1024, 512
    bpb = rows_per_batch // tm
    return pl.pallas_call(
        functools.partial(_inproj_kernel, sub=sub),
        out_shape=jax.ShapeDtypeStruct((m, n), F32),
        grid=(m // tm, n // tn),
        in_specs=[
            pl.BlockSpec((tm, d), lambda i, j: (i, 0)),
            pl.BlockSpec((None, 9, d), lambda i, j: (i // bpb, 0, 0)),
            pl.BlockSpec((3, d), lambda i, j: (0, 0)),
            pl.BlockSpec((d, tn), lambda i, j: (0, j)),
        ],
        out_specs=pl.BlockSpec((tm, tn), lambda i, j: (i, j)),
        scratch_shapes=[pltpu.VMEM((tm, d), BF16)],
        compiler_params=_cparams(("parallel", "arbitrary")),
        name="in_proj",
    )(x, mod_l, norm_g_l, w_pad)


def _outproj_kernel(x_ref, mod_ref, yn_ref, ym_ref, yr_ref, w_ref, o_ref, *, sub):
    acc = jnp.dot(yn_ref[...].astype(BF16), w_ref[0:NSA_W, :], preferred_element_type=F32)
    acc += jnp.dot(ym_ref[...].astype(BF16), w_ref[NSA_W:NSA_W + ML_W, :], preferred_element_type=F32)
    acc += jnp.dot(yr_ref[...].astype(BF16), w_ref[NSA_W + ML_W:, :], preferred_element_type=F32)
    o_ref[...] = x_ref[...] + mod_ref[3 * sub + 2:3 * sub + 3, :] * acc


def _outproj(x, mod_l, y_nsa, y_ml, y_rw, w_out, *, sub, rows_per_batch):
    m, d = x.shape
    tm = 256
    bpb = rows_per_batch // tm
    return pl.pallas_call(
        functools.partial(_outproj_kernel, sub=sub),
        out_shape=jax.ShapeDtypeStruct((m, d), F32),
        grid=(m // tm,),
        in_specs=[
            pl.BlockSpec((tm, d), lambda i: (i, 0)),
            pl.BlockSpec((None, 9, d), lambda i: (i // bpb, 0, 0)),
            pl.BlockSpec((tm, NSA_W), lambda i: (i, 0)),
            pl.BlockSpec((tm, ML_W), lambda i: (i, 0)),
            pl.BlockSpec((tm, RW_W), lambda i: (i, 0)),
            pl.BlockSpec((d, d), lambda i: (0, 0)),
        ],
        out_specs=pl.BlockSpec((tm, d), lambda i: (i, 0)),
        compiler_params=_cparams(("parallel",)),
        name="out_proj",
    )(x, mod_l, y_nsa, y_ml, y_rw, w_out)


ML_CHUNK = 256


def _shift_rows(x, tail, s, row):
    xs = pltpu.roll(x, s, 0)
    ts = pltpu.roll(tail, s, 0)
    top = jnp.where(row[:SUBLANES] < s, ts, xs[:SUBLANES])
    return jnp.concatenate([top, xs[SUBLANES:]], axis=0)


def _mlstm_kernel(zq_ref, zk_ref, zv_ref, zo_ref, zg_ref, gt_ref, cw_ref, cb_ref, gbc_ref, gbr_ref, ng_ref,
                  o_ref, c_scr, n_scr, m_scr, tq_scr, tk_scr):
    L = zq_ref.shape[0]
    hd = ML_HEAD_DIM

    @pl.when(pl.program_id(1) == 0)
    def _():
        c_scr[...] = jnp.zeros_like(c_scr)
        n_scr[...] = jnp.zeros_like(n_scr)
        m_scr[...] = jnp.zeros_like(m_scr)
        tq_scr[...] = jnp.zeros_like(tq_scr)
        tk_scr[...] = jnp.zeros_like(tk_scr)

    row = lax.broadcasted_iota(jnp.int32, (L, 1), 0)

    def conv(x_ref, tail_scr, w, b):
        x = x_ref[...]
        tail = tail_scr[...]
        acc = x * w[ML_CONV - 1:ML_CONV, :] + b
        for s in range(1, ML_CONV):
            acc = acc + _shift_rows(x, tail, s, row) * w[ML_CONV - 1 - s:ML_CONV - s, :]
        tail_scr[...] = x[L - SUBLANES:, :]
        return _silu(acc)

    cw = cw_ref[...]
    cb = cb_ref[...]
    q_all = conv(zq_ref, tq_scr, cw[:, :ML_W], cb[:, :ML_W])
    k_all = conv(zk_ref, tk_scr, cw[:, ML_W:], cb[:, ML_W:]) * (hd ** -0.5)

    g_col = zg_ref[...] + gbc_ref[...]
    g_row = gt_ref[...] + gbr_ref[...]
    ti = lax.broadcasted_iota(jnp.int32, (L, L), 0)
    si = lax.broadcasted_iota(jnp.int32, (L, L), 1)
    causal = si <= ti
    tril = causal.astype(F32)
    triu = (ti <= si).astype(F32)
    bcum_col = jnp.dot(tril, _log_sigmoid(g_col), precision=HIGHEST, preferred_element_type=F32)
    bcum_row = jnp.dot(_log_sigmoid(g_row), triu, precision=HIGHEST, preferred_element_type=F32)

    for h in range(ML_HEADS):
        sl = slice(h * hd, (h + 1) * hd)
        q = q_all[:, sl]
        k = k_all[:, sl]
        v = zv_ref[:, sl]
        i_col = g_col[:, h:h + 1]
        b_col = bcum_col[:, ML_HEADS + h:ML_HEADS + h + 1]
        i_row = g_row[h:h + 1, :]
        b_row = bcum_row[ML_HEADS + h:ML_HEADS + h + 1, :]
        m_prev = m_scr[h][:, 0:1]
        c_prev = c_scr[h]
        n_prev = n_scr[h]

        log_d = jnp.where(causal, b_col + (i_row - b_row), MASK_NEG)
        m_inter = b_col + m_prev
        m_t = jnp.maximum(jnp.max(log_d, axis=-1, keepdims=True), m_inter)
        dmat = jnp.exp(log_d - m_t)
        qb = q.astype(BF16)
        kb = k.astype(BF16)
        vb = v.astype(BF16)
        s = lax.dot_general(qb, kb, (((1,), (1,)), ((), ())), preferred_element_type=F32) * dmat
        w_inter = jnp.exp(m_inter - m_t)
        num = jnp.dot(s.astype(BF16), vb, preferred_element_type=F32)
        num += w_inter * jnp.dot(qb, c_prev.astype(BF16), preferred_element_type=F32)
        den = jnp.sum(s, axis=-1, keepdims=True) + w_inter * jnp.sum(q * n_prev, axis=-1, keepdims=True)
        hh = num / jnp.maximum(jnp.abs(den), jnp.exp(-m_t))

        g_last = b_row[:, L - 1:L]
        lw_row = g_last - b_row + i_row
        m_new = jnp.maximum(g_last + m_prev, jnp.max(lw_row, axis=-1, keepdims=True))
        wk_col = jnp.exp(g_last - b_col + i_col - m_new)
        dec = jnp.exp(g_last + m_prev - m_new)
        kw = k * wk_col
        c_scr[h] = dec * c_prev + lax.dot_general(kw.astype(BF16), vb, (((0,), (0,)), ((), ())),
                                                  preferred_element_type=F32)
        n_scr[h] = dec * n_prev + jnp.sum(kw, axis=0, keepdims=True)
        m_scr[h] = jnp.broadcast_to(m_new, (1, LANES))

        hh = hh * _sigmoid(zo_ref[:, sl])
        mu = jnp.mean(hh, axis=-1, keepdims=True)
        var = jnp.mean(jnp.square(hh - mu), axis=-1, keepdims=True)
        o_ref[:, sl] = (hh - mu) * lax.rsqrt(var + NORM_EPS) * ng_ref[:, sl]


def _mlstm(z_pad, gates_t, conv_w, conv_b, gate_b, norm_g, *, batch, seq):
    L = min(ML_CHUNK, seq)
    nc = seq // L
    m = batch * seq
    gb = gate_b.reshape(2 * ML_HEADS)
    gb_col = jnp.zeros((1, LANES), F32).at[0, :2 * ML_HEADS].set(gb)
    gb_row = jnp.broadcast_to(gb[:, None], (2 * ML_HEADS, L))
    cb = lambda j: pl.BlockSpec((L, ML_W), lambda b, c: (b * nc + c, _Z_ML // ML_W + j))
    return pl.pallas_call(
        _mlstm_kernel,
        out_shape=jax.ShapeDtypeStruct((m, ML_W), F32),
        grid=(batch, nc),
        in_specs=[
            cb(0), cb(1), cb(2), cb(3),
            pl.BlockSpec((L, LANES), lambda b, c: (b * nc + c, _Z_ML_GATE // LANES)),
            pl.BlockSpec((2 * ML_HEADS, L), lambda b, c: (0, b * nc + c)),
            pl.BlockSpec((ML_CONV, 2 * ML_W), lambda b, c: (0, 0)),
            pl.BlockSpec((1, 2 * ML_W), lambda b, c: (0, 0)),
            pl.BlockSpec((1, LANES), lambda b, c: (0, 0)),
            pl.BlockSpec((2 * ML_HEADS, L), lambda b, c: (0, 0)),
            pl.BlockSpec((1, ML_W), lambda b, c: (0, 0)),
        ],
        out_specs=pl.BlockSpec((L, ML_W), lambda b, c: (b * nc + c, 0)),
        scratch_shapes=[
            pltpu.VMEM((ML_HEADS, ML_HEAD_DIM, ML_HEAD_DIM), F32),
            pltpu.VMEM((ML_HEADS, 1, ML_HEAD_DIM), F32),
            pltpu.VMEM((ML_HEADS, 1, LANES), F32),
            pltpu.VMEM((SUBLANES, ML_W), F32),
            pltpu.VMEM((SUBLANES, ML_W), F32),
        ],
        compiler_params=_cparams(("parallel", "arbitrary")),
        name="mlstm",
    )(z_pad, z_pad, z_pad, z_pad, z_pad, gates_t, conv_w, conv_b.reshape(1, -1), gb_col, gb_row,
      norm_g.reshape(1, -1))


RW_CHUNK = 64
RW_PAIRS = RW_HEADS // 2


def _dot_hi(a, b):
    return jnp.dot(a, b, precision=HIGHEST, preferred_element_type=F32)


def _dot_nt_hi(a, b):
    return lax.dot_general(a, b, (((1,), (1,)), ((), ())), precision=HIGHEST, preferred_element_type=F32)


def _dot_tn_hi(a, b):
    return lax.dot_general(a, b, (((0,), (0,)), ((), ())), precision=HIGHEST, preferred_element_type=F32)


def _rwkv_kernel(zr_ref, zk_ref, zv_ref, zwa_ref, zgd_ref, mur_ref, muk_ref, muv_ref, muwa_ref, mugd_ref,
                 w0_ref, wup_ref, a0_ref, aup_ref, gup_ref, kk_ref, ka_ref, rk_ref, ln_ref, ones_ref,
                 o_ref, s_scr, tr_scr, tk_scr, tv_scr, twa_scr, tgd_scr):
    C = zr_ref.shape[0]
    hd = RW_HEAD_DIM

    @pl.when(pl.program_id(1) == 0)
    def _():
        s_scr[...] = jnp.zeros_like(s_scr)
        tr_scr[...] = jnp.zeros_like(tr_scr)
        tk_scr[...] = jnp.zeros_like(tk_scr)
        tv_scr[...] = jnp.zeros_like(tv_scr)
        twa_scr[...] = jnp.zeros_like(twa_scr)
        tgd_scr[...] = jnp.zeros_like(tgd_scr)

    row = lax.broadcasted_iota(jnp.int32, (C, 1), 0)

    def shift(x_ref, t_scr, mu_ref):
        x = x_ref[...]
        xp = jnp.where(row == 0, t_scr[...], pltpu.roll(x, 1, 0))
        t_scr[...] = x[C - 1:C, :]
        return x + mu_ref[...] * (xp - x)

    zr = shift(zr_ref, tr_scr, mur_ref)
    zk = shift(zk_ref, tk_scr, muk_ref)
    zv = shift(zv_ref, tv_scr, muv_ref)
    zwa = shift(zwa_ref, twa_scr, muwa_ref)
    zgd = shift(zgd_ref, tgd_scr, mugd_ref)

    ones = ones_ref[...]
    u = -(w0_ref[...] + jnp.dot(jnp.tanh(zwa).astype(BF16), wup_ref[...], preferred_element_type=F32))
    softplus = jnp.maximum(u, 0.0) + jnp.log(1.0 + jnp.exp(-jnp.abs(u)))
    logw = -jnp.exp(-softplus - 0.5)
    a = _sigmoid(a0_ref[...] + jnp.dot(zwa.astype(BF16), aup_ref[...], preferred_element_type=F32))
    g = jnp.dot(_sigmoid(zgd).astype(BF16), gup_ref[...], preferred_element_type=F32)
    kk = zk * kk_ref[...]
    kk = kk / jnp.maximum(jnp.sqrt(_dot_hi(kk * kk, ones)), 1e-12)
    k2 = zk * (1.0 + (a - 1.0) * ka_ref[...])
    bv = kk * a

    tc = lax.broadcasted_iota(jnp.int32, (C, C), 0)
    sc = lax.broadcasted_iota(jnp.int32, (C, C), 1)
    lc = _dot_hi((sc <= tc).astype(F32), logw)
    l_last = lc[C - 1:C, :]
    e_neg = jnp.exp(-lc)
    e_end = jnp.exp(l_last - lc)
    at = -kk * jnp.exp(lc - logw)
    rt = zr * jnp.exp(lc)
    bt = bv * e_neg
    kt = k2 * e_neg
    bte = bv * e_end
    kte = k2 * e_end
    wc = jnp.exp(l_last)

    lane = lax.broadcasted_iota(jnp.int32, (1, LANES), 1)
    head_a = lane < hd
    t2 = lax.broadcasted_iota(jnp.int32, (2 * C, 2 * C), 0)
    s2 = lax.broadcasted_iota(jnp.int32, (2 * C, 2 * C), 1)
    same = (t2 < C) == (s2 < C)
    strict = same & (s2 < t2)
    incl = same & (s2 <= t2)
    eye = (s2 == t2).astype(F32)

    def stack_masked(x):
        return jnp.concatenate([jnp.where(head_a, x, 0.0), jnp.where(head_a, 0.0, x)], axis=0)

    def stack(x):
        return jnp.concatenate([x, x], axis=0)

    ys = []
    for p in range(RW_PAIRS):
        sl = slice(p * LANES, (p + 1) * LANES)
        at_s = stack_masked(at[:, sl])
        rt_s = stack_masked(rt[:, sl])
        b_s = stack(bt[:, sl])
        k_s = stack(kt[:, sl])
        v_s = stack_masked(zv[:, sl])
        a_ab = jnp.where(strict, _dot_nt_hi(at_s, b_s), 0.0)
        a_ak = jnp.where(strict, _dot_nt_hi(at_s, k_s), 0.0)
        b_rb = jnp.where(incl, _dot_nt_hi(rt_s, b_s), 0.0)
        b_rk = jnp.where(incl, _dot_nt_hi(rt_s, k_s), 0.0)
        x = a_ab
        tinv = eye + a_ab
        for _ in range(int(math.log2(C)) - 1):
            x = _dot_hi(x, x)
            tinv = tinv + _dot_hi(tinv, x)
        s0 = s_scr[p]
        uu = _dot_hi(tinv, _dot_nt_hi(at_s, s0) + _dot_hi(a_ak, v_s))
        yy = _dot_nt_hi(rt_s, s0) + _dot_hi(b_rb, uu) + _dot_hi(b_rk, v_s)
        ys.append(yy[:C] + yy[C:])
        s_scr[p] = (s0 * wc[:, sl] + _dot_tn_hi(uu, stack_masked(bte[:, sl]))
                    + _dot_tn_hi(v_s, stack_masked(kte[:, sl])))

    y = jnp.concatenate(ys, axis=1)
    inv_hd = 1.0 / hd
    mean = _dot_hi(y, ones) * inv_hd
    yc = y - mean
    var = _dot_hi(yc * yc, ones) * inv_hd
    yn = yc * lax.rsqrt(var + RW_LN_EPS) * ln_ref[0:1, :] + ln_ref[1:2, :]
    bonus = _dot_hi(zr * k2 * rk_ref[...], ones) * zv
    o_ref[...] = (yn + bonus) * g


def _rwkv(z_pad, mu, w0, w_up, a0, a_up, g_up, k_k, k_a, r_k, ln, *, batch, seq):
    C = RW_CHUNK
    nc = seq // C
    m = batch * seq
    w = RW_W
    row = lambda v: v.reshape(1, -1).astype(F32)
    wup_pad = jnp.zeros((LANES, w), F32).at[:RW_DECAY_RANK].set(w_up).astype(BF16)
    aup_pad = jnp.zeros((LANES, w), F32).at[RW_DECAY_RANK:].set(a_up).astype(BF16)
    head_of = np.arange(w) // RW_HEAD_DIM
    ones = jnp.asarray((head_of[:, None] == head_of[None, :]).astype(np.float32))
    zspec = lambda width, off: pl.BlockSpec((C, width), lambda b, c: (b * nc + c, off // width))
    full = lambda shape: pl.BlockSpec(shape, lambda b, c: tuple(0 for _ in shape))
    return pl.pallas_call(
        _rwkv_kernel,
        out_shape=jax.ShapeDtypeStruct((m, w), F32),
        grid=(batch, nc),
        in_specs=[
            zspec(w, _Z_RW), zspec(w, _Z_RW + w), zspec(w, _Z_RW + 2 * w),
            zspec(LANES, _Z_RW_WA), zspec(LANES, _Z_RW_GD),
            full((1, w)), full((1, w)), full((1, w)), full((1, LANES)), full((1, LANES)),
            full((1, w)), full((LANES, w)), full((1, w)), full((LANES, w)), full((LANES, w)),
            full((1, w)), full((1, w)), full((1, w)), full((2, w)), full((w, w)),
        ],
        out_specs=pl.BlockSpec((C, w), lambda b, c: (b * nc + c, 0)),
        scratch_shapes=[
            pltpu.VMEM((RW_PAIRS, LANES, LANES), F32),
            pltpu.VMEM((1, w), F32), pltpu.VMEM((1, w), F32), pltpu.VMEM((1, w), F32),
            pltpu.VMEM((1, LANES), F32), pltpu.VMEM((1, LANES), F32),
        ],
        compiler_params=_cparams(("parallel", "arbitrary")),
        name="rwkv7",
    )(z_pad, z_pad, z_pad, z_pad, z_pad,
      row(mu[:w]), row(mu[w:2 * w]), row(mu[2 * w:3 * w]), row(mu[3 * w:3 * w + LANES]), row(mu[3 * w + LANES:]),
      row(w0), wup_pad, row(a0), aup_pad, g_up.astype(BF16), row(k_k), row(k_a), row(r_k), ln.astype(F32), ones)


CMP_PER_Q = Q_BLOCK // CMP_STRIDE
CMP_PER_SEL = SEL_BLOCK // CMP_STRIDE
QG = NSA_GROUP * Q_BLOCK
FAR_BUCKET = REL_BUCKETS - 1
NSA_TILE = 2 * SEL_BLOCK
NSA_CHUNK = 4 * NSA_TILE
NSA_V_ROWS = NSA_HEAD_DIM + 16


def _bucket_table(n_max):
    n = np.arange(n_max)
    max_exact = REL_BUCKETS // 2
    def large(dt):
        v = np.log(np.maximum(n, max_exact).astype(dt) / dt(max_exact)) / dt(math.log(REL_MAX_DIST / max_exact))
        return max_exact + (v * dt(REL_BUCKETS - max_exact)).astype(np.int32)
    l32, l64 = large(np.float32), large(np.float64)
    assert np.array_equal(np.minimum(l32, FAR_BUCKET), np.minimum(l64, FAR_BUCKET))
    return np.where(n < max_exact, n, np.minimum(l64, FAR_BUCKET)).astype(np.int32)


_FAR_DIST = int(np.argmax(_bucket_table(4096) == FAR_BUCKET))
assert np.all(_bucket_table(4096)[_FAR_DIST:] == FAR_BUCKET) and _FAR_DIST <= Q_BLOCK - CMP_STRIDE + 1


def _rel_bias_tile(rel_bias, dist, valid):
    table = _bucket_table(int(dist.max()) + 2)
    idx = table[np.clip(dist, 0, None)]
    tile = rel_bias.astype(F32)[:, idx]
    tile = jnp.where(jnp.asarray(valid)[None], tile, MASK_NEG)
    rows = dist.shape[0]
    tile = tile.reshape(NSA_KV_HEADS, NSA_GROUP, rows, Q_BLOCK).transpose(0, 2, 1, 3)
    return tile.reshape(NSA_KV_HEADS, rows, QG)


def _compress_kernel(r_ref, pos_ref, w1_ref, w2_ref, o_ref):
    nc = r_ref.shape[0]
    half = CMP_STRIDE * NSA_HEAD_DIM
    r = r_ref[...]
    w1 = w1_ref[...].astype(BF16)
    p_top = pos_ref[0:1, :]
    p_bot = pos_ref[1:2, :]
    g1 = jnp.dot((r + p_top).astype(BF16), w1[:half], preferred_element_type=F32)
    g2 = jnp.dot((r + p_bot).astype(BF16), w1[half:], preferred_element_type=F32)
    tail = jnp.dot(jnp.broadcast_to(p_bot, (SUBLANES, half)).astype(BF16), w1[half:],
                   preferred_element_type=F32)[0:1]
    row = lax.broadcasted_iota(jnp.int32, (nc, 1), 0)
    g2_next = jnp.where(row == nc - 1, tail, pltpu.roll(g2, nc - 1, 0))
    hid = _silu(g1 + g2_next)
    o_ref[...] = jnp.dot(hid.astype(BF16), w2_ref[...].astype(BF16), preferred_element_type=F32)


def _compress(r_kv, cmp_pos, cmp_w1, cmp_w2):
    two, bh, nc, width = r_kv.shape
    pos = cmp_pos.reshape(2, 2, width)
    return pl.pallas_call(
        _compress_kernel,
        out_shape=jax.ShapeDtypeStruct((2, bh, nc, NSA_HEAD_DIM), F32),
        grid=(2, bh),
        in_specs=[
            pl.BlockSpec((None, None, nc, width), lambda s, i: (s, i, 0, 0)),
            pl.BlockSpec((None, 2, width), lambda s, i: (s, 0, 0)),
            pl.BlockSpec((None, 2 * width, CMP_HIDDEN), lambda s, i: (s, 0, 0)),
            pl.BlockSpec((None, CMP_HIDDEN, NSA_HEAD_DIM), lambda s, i: (s, 0, 0)),
        ],
        out_specs=pl.BlockSpec((None, None, nc, NSA_HEAD_DIM), lambda s, i: (s, i, 0, 0)),
        compiler_params=_cparams(("parallel", "parallel")),
        name="nsa_compress",
    )(r_kv, pos, cmp_w1, cmp_w2)


def _nsa_cmp_kernel(qt_ref, kc_ref, vct_ref, far_ref, delta_ref, ocmp_ref, sel_ref, flag_ref, s_scr, imp_scr,
                    *, n_sel):
    qb = pl.program_id(2)
    nc = kc_ref.shape[0]
    nb = nc // CMP_PER_SEL
    pad = SUBLANES

    s = jnp.dot(kc_ref[...].astype(BF16), qt_ref[...], preferred_element_type=F32) + far_ref[...]
    s_scr[0:pad, :] = jnp.zeros((pad, QG), F32)
    s_scr[pad:, :] = s
    win = pl.ds(pl.multiple_of(CMP_PER_Q * qb, SUBLANES), 2 * CMP_PER_Q)
    s_scr[win, :] = s_scr[win, :] + delta_ref[...]
    s = s_scr[pad:, :]

    n_idx = lax.broadcasted_iota(jnp.int32, (nc, 1), 0)
    lane = lax.broadcasted_iota(jnp.int32, (1, QG), 1)
    t_q = qb * Q_BLOCK + (lane & (Q_BLOCK - 1))
    valid = n_idx * CMP_STRIDE + (CMP_LEN - 1) <= t_q
    m = jnp.max(jnp.where(valid, s, MASK_NEG), axis=0, keepdims=True)
    p = jnp.where(valid, jnp.exp(s - m), 0.0)
    l = jnp.sum(p, axis=0, keepdims=True)
    pn = p * jnp.where(l > 0.0, 1.0 / l, 0.0)
    ocmp_ref[...] = jnp.dot(vct_ref[...].astype(BF16), pn.astype(BF16), preferred_element_type=F32)

    imp = pn[:, 0:Q_BLOCK]
    for g in range(1, NSA_GROUP):
        imp = imp + pn[:, g * Q_BLOCK:(g + 1) * Q_BLOCK]
    imp_scr[...] = imp
    parts = [imp_scr[pl.ds(c, nb, stride=CMP_PER_SEL), :] for c in range(CMP_PER_SEL)]
    score = parts[0]
    for c in range(1, CMP_PER_SEL):
        score = score + parts[c]
    blk = lax.broadcasted_iota(jnp.int32, (nb, 1), 0)
    score = score + jnp.where(blk == 0, 0.0, pltpu.roll(parts[-1], 1, 0))

    t = qb * Q_BLOCK + lax.broadcasted_iota(jnp.int32, (1, Q_BLOCK), 1)
    cur = t // SEL_BLOCK
    forced = (blk == 0) | (blk == cur) | (blk == cur - 1)
    ok = blk * SEL_BLOCK <= t
    score = jnp.where(forced, -MASK_NEG, jnp.where(ok, score, MASK_NEG))
    sel = jnp.zeros((nb, Q_BLOCK), F32)
    for _ in range(n_sel):
        top = jnp.max(score, axis=0, keepdims=True)
        first = jnp.min(jnp.where(score == top, blk, nb), axis=0, keepdims=True)
        hit = blk == first
        sel = jnp.where(hit, 1.0, sel)
        score = jnp.where(hit, -jnp.inf, score)
    sel = jnp.where(ok, sel, 0.0)
    sel_ref[...] = sel
    cnt = lax.dot_general(jnp.ones((SUBLANES, Q_BLOCK), F32), sel, (((1,), (1,)), ((), ())),
                          preferred_element_type=F32)
    flag_ref[...] = (cnt[0:1, :] > 0.0).astype(jnp.int32)


def _nsa_attn_kernel(flag_ref, qt_ref, sel_ref, ocmp_ref, gate_ref, ks_ref, vst_ref, kw_ref, vwt_ref,
                     qaug_ref, bdiag_ref, bwin_ref, o_ref, q_scr, mb_scr, m_scr, acc_scr):
    qb = pl.program_id(2)
    hd = NSA_HEAD_DIM
    per_tile = NSA_TILE // SEL_BLOCK
    blocks_per_chunk = NSA_CHUNK // SEL_BLOCK
    tiles_per_chunk = NSA_CHUNK // NSA_TILE

    q_scr[0:hd, :] = qt_ref[...]
    q_scr[hd:, :] = qaug_ref[...]
    mb = (sel_ref[...] - 1.0) * (-MASK_NEG)
    mb_scr[...] = jnp.concatenate([mb] * NSA_GROUP, axis=1)

    def reset():
        m_scr[...] = jnp.full_like(m_scr, MASK_NEG)
        acc_scr[...] = jnp.zeros_like(acc_scr)

    def update(k_ref, vt_ref, start, n_keys, bias=None, first_block=None):
        start = pl.multiple_of(start, NSA_TILE)
        s = jnp.dot(k_ref[pl.ds(start, n_keys), :], q_scr[...], preferred_element_type=F32)
        if bias is not None:
            s = s + bias
        if first_block is not None:
            n_blk = n_keys // SEL_BLOCK
            if n_blk % SUBLANES == 0:
                rows = mb_scr[pl.ds(first_block, n_blk), :]
                rows = [rows[j:j + 1] for j in range(n_blk)]
            else:
                rows = [mb_scr[pl.ds(first_block + j, 1), :] for j in range(n_blk)]
            s = jnp.concatenate([s[j * SEL_BLOCK:(j + 1) * SEL_BLOCK] + rows[j] for j in range(n_blk)], axis=0)
        m_old = m_scr[...]
        m_new = jnp.maximum(m_old, jnp.max(s, axis=0, keepdims=True))
        alpha = jnp.exp(m_old - m_new)
        p = jnp.exp(s - m_new).astype(BF16)
        acc_scr[...] = alpha * acc_scr[...] + jnp.dot(vt_ref[:, pl.ds(start, n_keys)], p,
                                                      preferred_element_type=F32)
        m_scr[...] = m_new

    def finish():
        return acc_scr[0:hd, :] * (1.0 / acc_scr[hd:hd + 1, :])

    reset()
    n_far = jnp.maximum(qb - 1, 0)
    n_chunks = n_far // tiles_per_chunk

    def chunk_body(c, carry):
        active = flag_ref[0, blocks_per_chunk * c]
        for j in range(1, blocks_per_chunk):
            active = active + flag_ref[0, blocks_per_chunk * c + j]

        @pl.when(active > 0)
        def _():
            update(ks_ref, vst_ref, c * NSA_CHUNK, NSA_CHUNK,
                   first_block=pl.multiple_of(c * blocks_per_chunk, blocks_per_chunk))
        return carry

    lax.fori_loop(0, n_chunks, chunk_body, 0)

    for j in range(tiles_per_chunk - 1):
        tile = n_chunks * tiles_per_chunk + j
        tile_c = jnp.minimum(tile, n_far)
        active = flag_ref[0, per_tile * tile_c] + flag_ref[0, per_tile * tile_c + 1]

        @pl.when((tile < n_far) & (active > 0))
        def _():
            update(ks_ref, vst_ref, tile * NSA_TILE, NSA_TILE, first_block=tile * per_tile)

    @pl.when(qb > 0)
    def _():
        update(ks_ref, vst_ref, (qb - 1) * NSA_TILE, NSA_TILE, bias=bdiag_ref[0:NSA_TILE, :],
               first_block=(qb - 1) * per_tile)

    update(ks_ref, vst_ref, qb * NSA_TILE, NSA_TILE, bias=bdiag_ref[NSA_TILE:, :], first_block=qb * per_tile)
    o_slc = finish()

    reset()
    n_win = WINDOW // NSA_TILE + 1
    for j in range(n_win):
        tile = qb - (n_win - 1) + j
        if j == n_win - 1:
            update(kw_ref, vwt_ref, tile * NSA_TILE, NSA_TILE, bias=bwin_ref[j * NSA_TILE:(j + 1) * NSA_TILE, :])
        else:
            @pl.when(tile >= 0)
            def _():
                update(kw_ref, vwt_ref, tile * NSA_TILE, NSA_TILE,
                       bias=bwin_ref[j * NSA_TILE:(j + 1) * NSA_TILE, :])
    o_win = finish()

    sig = _sigmoid(gate_ref[...])

    def gate_row(branch):
        return jnp.concatenate([sig[g * 3 + branch:g * 3 + branch + 1, :] for g in range(NSA_GROUP)], axis=1)

    o_t = gate_row(0) * ocmp_ref[...] + gate_row(1) * o_slc + gate_row(2) * o_win
    stacked = jnp.concatenate([o_t[:, g * Q_BLOCK:(g + 1) * Q_BLOCK] for g in range(NSA_GROUP)], axis=0)
    o_ref[...] = stacked.T


def _nsa(z_pad, rel_bias, cmp_pos, cmp_w1, cmp_w2, *, batch, seq):
    kvh, grp, hd = NSA_KV_HEADS, NSA_GROUP, NSA_HEAD_DIM
    bh = batch * kvh
    nqb = seq // Q_BLOCK
    nc = seq // CMP_STRIDE
    nb = seq // SEL_BLOCK
    n_sel = min(N_SEL, nb)
    m = batch * seq

    zq = z_pad[:, _Z_NSA_Q:_Z_NSA_Q + NSA_W].reshape(batch, nqb, Q_BLOCK, kvh, grp, hd)
    qt = (zq * hd ** -0.5).astype(BF16).transpose(0, 3, 1, 5, 4, 2).reshape(bh, nqb, hd, QG)
    kv = z_pad[:, _Z_NSA_KV:_Z_NSA_KV + 6 * NSA_KV_W].reshape(batch, seq, 6, kvh, hd)
    kv = kv.transpose(2, 0, 3, 1, 4).reshape(6, bh, seq, hd)
    r_kv = kv[0:2].reshape(2, bh, nc, CMP_STRIDE * hd)
    def k_aug(k, fold):
        extra = jnp.full((bh, seq, 2), fold, BF16)
        return jnp.concatenate([k.astype(BF16), extra, jnp.zeros((bh, seq, LANES - hd - 2), BF16)], axis=-1)

    def vt_aug(v):
        return jnp.concatenate([v.astype(BF16).transpose(0, 2, 1), jnp.ones((bh, 1, seq), BF16),
                                jnp.zeros((bh, NSA_V_ROWS - hd - 1, seq), BF16)], axis=1)

    ks = k_aug(kv[2], 1.0)
    vst = vt_aug(kv[3])
    kw = k_aug(kv[4], 0.0)
    vwt = vt_aug(kv[5])
    gates_t = z_pad[:, _Z_NSA_GT:_Z_NSA_GT + 3 * NSA_HEADS].reshape(batch, seq, kvh, grp * 3)
    gates_t = gates_t.transpose(0, 2, 3, 1).reshape(bh, grp * 3, seq)

    qi = np.arange(Q_BLOCK)[None, :]
    far = _rel_bias_tile(rel_bias, np.full((1, Q_BLOCK), _FAR_DIST), np.ones((1, Q_BLOCK), bool))
    mrow = np.arange(-CMP_PER_Q, CMP_PER_Q)[:, None]
    d_cmp = qi - CMP_STRIDE * mrow - (CMP_LEN - 1)
    delta = _rel_bias_tile(rel_bias, d_cmp, d_cmp >= 0)
    delta = jnp.where(jnp.asarray(np.tile(d_cmp >= 0, (1, grp)))[None], delta - far, 0.0)
    d_diag = qi + NSA_TILE - np.arange(2 * NSA_TILE)[:, None]
    bdiag = _rel_bias_tile(rel_bias, d_diag, d_diag >= 0) - far
    far_hi = far.astype(BF16)
    far_lo = (far - far_hi.astype(F32)).astype(BF16)
    q_aug = jnp.concatenate([far_hi, far_lo, jnp.zeros((kvh, LANES - hd - 2, QG), BF16)], axis=1)
    d_win = qi + WINDOW - np.arange(WINDOW + NSA_TILE)[:, None]
    bwin = _rel_bias_tile(rel_bias, d_win, (d_win >= 0) & (d_win < WINDOW))

    cmp_kv = _compress(r_kv, cmp_pos, cmp_w1, cmp_w2)
    k_cmp = cmp_kv[0]
    v_cmp_t = cmp_kv[1].transpose(0, 2, 1)

    g3 = (batch, kvh, nqb)
    bhq = lambda b, h, q: (b * kvh + h, q, 0, 0)
    bh0 = lambda b, h, q: (b * kvh + h, 0, 0)
    per_h = lambda b, h, q: (h, 0, 0)
    o_cmp, sel, flags = pl.pallas_call(
        functools.partial(_nsa_cmp_kernel, n_sel=n_sel),
        out_shape=(jax.ShapeDtypeStruct((bh, nqb, hd, QG), F32),
                   jax.ShapeDtypeStruct((bh, nqb, nb, Q_BLOCK), F32),
                   jax.ShapeDtypeStruct((bh, nqb, 1, nb), jnp.int32)),
        grid=g3,
        in_specs=[
            pl.BlockSpec((None, None, hd, QG), bhq),
            pl.BlockSpec((None, nc, hd), bh0),
            pl.BlockSpec((None, hd, nc), bh0),
            pl.BlockSpec((None, 1, QG), per_h),
            pl.BlockSpec((None, 2 * CMP_PER_Q, QG), per_h),
        ],
        out_specs=(pl.BlockSpec((None, None, hd, QG), bhq),
                   pl.BlockSpec((None, None, nb, Q_BLOCK), bhq),
                   pl.BlockSpec((None, None, 1, nb), bhq)),
        scratch_shapes=[pltpu.VMEM((nc + SUBLANES, QG), F32), pltpu.VMEM((nc, Q_BLOCK), F32)],
        compiler_params=_cparams(("parallel", "parallel", "arbitrary")),
        name="nsa_cmp_select",
    )(qt, k_cmp, v_cmp_t, far, delta)

    return pl.pallas_call(
        _nsa_attn_kernel,
        out_shape=jax.ShapeDtypeStruct((m, NSA_W), F32),
        grid=g3,
        in_specs=[
            pl.BlockSpec((None, None, 1, nb), bhq, memory_space=pltpu.SMEM),
            pl.BlockSpec((None, None, hd, QG), bhq),
            pl.BlockSpec((None, None, nb, Q_BLOCK), bhq),
            pl.BlockSpec((None, None, hd, QG), bhq),
            pl.BlockSpec((None, grp * 3, Q_BLOCK), lambda b, h, q: (b * kvh + h, 0, q)),
            pl.BlockSpec((None, seq, LANES), bh0),
            pl.BlockSpec((None, NSA_V_ROWS, seq), bh0),
            pl.BlockSpec((None, seq, LANES), bh0),
            pl.BlockSpec((None, NSA_V_ROWS, seq), bh0),
            pl.BlockSpec((None, LANES - hd, QG), per_h),
            pl.BlockSpec((None, 2 * NSA_TILE, QG), per_h),
            pl.BlockSpec((None, WINDOW + NSA_TILE, QG), per_h),
        ],
        out_specs=pl.BlockSpec((Q_BLOCK, grp * hd), lambda b, h, q: (b * nqb + q, h)),
        scratch_shapes=[pltpu.VMEM((LANES, QG), BF16), pltpu.VMEM((nb, QG), F32), pltpu.VMEM((1, QG), F32),
                        pltpu.VMEM((NSA_V_ROWS, QG), F32)],
        compiler_params=_cparams(("parallel", "parallel", "arbitrary")),
        name="nsa_attention",
    )(flags, qt, sel, o_cmp, gates_t, ks, vst, kw, vwt, q_aug, bdiag, bwin)


def kernel(x, c, rel_bias, final_norm, ada_w, ada_b, norm_g, ffn_w_gate, ffn_w_up, ffn_w_down, w_in, w_out,
           cmp_pos, cmp_w1, cmp_w2, ml_conv_w, ml_conv_b, ml_gate_b, ml_norm, rw_mu, rw_w0, rw_w_up, rw_a0,
           rw_a_up, rw_g_up, rw_k_k, rw_k_a, rw_r_k, rw_ln):
    b_, t_, d = x.shape
    m = b_ * t_
    mod = _adaln_mod(c, ada_w, ada_b)
    src = _z_column_map()
    col_ok = jnp.asarray(src >= 0)
    col_src = np.maximum(src, 0)
    fin = final_norm.reshape(1, d)
    xf = x.reshape(m, d)
    for l in range(DEPTH):
        wg = ffn_w_gate[l].astype(BF16)
        wu = ffn_w_up[l].astype(BF16)
        wd = ffn_w_down[l].astype(BF16)
        w_pad = jnp.where(col_ok[None, :], w_in[l][:, col_src], 0.0).astype(BF16)
        xf = _ffn(xf, mod[l], norm_g[l], wg[0], wu[0], wd[0], fin, sub=0, rows_per_batch=t_, final=False)
        z_pad = _inproj(xf, mod[l], norm_g[l], w_pad, sub=1, rows_per_batch=t_)
        gates_t = z_pad[:, _Z_ML_GATE:_Z_ML_GATE + 2 * ML_HEADS].T
        y_ml = _mlstm(z_pad, gates_t, ml_conv_w[l], ml_conv_b[l], ml_gate_b[l], ml_norm[l], batch=b_, seq=t_)
        y_nsa = _nsa(z_pad, rel_bias, cmp_pos[l], cmp_w1[l], cmp_w2[l], batch=b_, seq=t_)
        y_rw = _rwkv(z_pad, rw_mu[l], rw_w0[l], rw_w_up[l], rw_a0[l], rw_a_up[l], rw_g_up[l], rw_k_k[l],
                     rw_k_a[l], rw_r_k[l], rw_ln[l], batch=b_, seq=t_)
        xf = _outproj(xf, mod[l], y_nsa, y_ml, y_rw, w_out[l].astype(BF16), sub=1, rows_per_batch=t_)
        xf = _ffn(xf, mod[l], norm_g[l], wg[1], wu[1], wd[1], fin, sub=2, rows_per_batch=t_,
                  final=(l == DEPTH - 1))
    return xf.reshape(b_, t_, d)
```

```python
import functools
import math

import numpy as np
import jax
import jax.numpy as jnp
from jax import lax
from jax.experimental import pallas as pl
from jax.experimental.pallas import tpu as pltpu

D_MODEL = 2048
DEPTH = 2
NSA_HEADS = 16
NSA_KV_HEADS = 4
NSA_HEAD_DIM = 64
NSA_GROUP = NSA_HEADS // NSA_KV_HEADS
CMP_LEN = 32
CMP_STRIDE = 16
CMP_HIDDEN = 128
SEL_BLOCK = 64
N_SEL = 16
WINDOW = 512
Q_BLOCK = 128
ML_HEADS = 4
ML_HEAD_DIM = 128
ML_CONV = 4
RW_HEADS = 8
RW_HEAD_DIM = 64
RW_DECAY_RANK = 64
RW_AICL_RANK = 64
RW_GATE_RANK = 128
D_FF = 5632
REL_BUCKETS = 32
REL_MAX_DIST = 128
NORM_EPS = 1e-6
RW_LN_EPS = 64e-5
MASK_NEG = -1e30

NSA_W = NSA_HEADS * NSA_HEAD_DIM
NSA_KV_W = NSA_KV_HEADS * NSA_HEAD_DIM
ML_W = ML_HEADS * ML_HEAD_DIM
RW_W = RW_HEADS * RW_HEAD_DIM
NSA_COLS = NSA_W + 6 * NSA_KV_W + 3 * NSA_HEADS
ML_COLS = 4 * ML_W + 2 * ML_HEADS
RW_COLS = 3 * RW_W + RW_DECAY_RANK + RW_AICL_RANK + RW_GATE_RANK

LANES = 128
SUBLANES = 8
VMEM_LIMIT = 56 * 1024 * 1024

F32 = jnp.float32
BF16 = jnp.bfloat16
HIGHEST = lax.Precision.HIGHEST

_Z_NSA_Q = 0
_Z_NSA_KV = 1024
_Z_NSA_GT = 2560
_Z_ML_GATE = 2688
_Z_RW_WA = 2816
_Z_RW_GD = 2944
_Z_ML = 3072
_Z_RW = 5120
Z_PAD = 6656


def _z_column_map():
    src = np.full((Z_PAD,), -1, np.int64)
    o_ml = NSA_COLS
    o_rw = NSA_COLS + ML_COLS
    src[_Z_NSA_Q:_Z_NSA_Q + NSA_W] = np.arange(NSA_W)
    src[_Z_NSA_KV:_Z_NSA_KV + 6 * NSA_KV_W] = NSA_W + np.arange(6 * NSA_KV_W)
    src[_Z_NSA_GT:_Z_NSA_GT + 3 * NSA_HEADS] = NSA_W + 6 * NSA_KV_W + np.arange(3 * NSA_HEADS)
    src[_Z_ML_GATE:_Z_ML_GATE + 2 * ML_HEADS] = o_ml + 4 * ML_W + np.arange(2 * ML_HEADS)
    src[_Z_RW_WA:_Z_RW_WA + 128] = o_rw + 3 * RW_W + np.arange(128)
    src[_Z_RW_GD:_Z_RW_GD + 128] = o_rw + 3 * RW_W + 128 + np.arange(128)
    src[_Z_ML:_Z_ML + 4 * ML_W] = o_ml + np.arange(4 * ML_W)
    src[_Z_RW:_Z_RW + 3 * RW_W] = o_rw + np.arange(3 * RW_W)
    return src


def _cparams(sem, vmem=VMEM_LIMIT, flags=None):
    return pltpu.CompilerParams(dimension_semantics=sem, vmem_limit_bytes=vmem, flags=flags)


def _sigmoid(x):
    return 1.0 / (1.0 + jnp.exp(-x))


def _silu(x):
    return x * _sigmoid(x)


def _log_sigmoid(x):
    return -(jnp.maximum(-x, 0.0) + jnp.log(1.0 + jnp.exp(-jnp.abs(x))))


def _rms_mod(x, g, shift, scale):
    ms = jnp.mean(x * x, axis=-1, keepdims=True)
    y = x * lax.rsqrt(ms + NORM_EPS) * g
    return y * (1.0 + scale) + shift


def _mod_kernel(c_ref, w_ref, b_ref, o_ref):
    cond = _silu(c_ref[...]).astype(BF16)
    o_ref[...] = jnp.dot(cond, w_ref[...].astype(BF16), preferred_element_type=F32) + b_ref[...]


def _adaln_mod(c, ada_w, ada_b):
    b_, d = c.shape
    depth, _, n = ada_w.shape
    tn = 1024
    c8 = jnp.zeros((SUBLANES, d), F32).at[:b_].set(c)
    out = pl.pallas_call(
        _mod_kernel,
        out_shape=jax.ShapeDtypeStruct((depth, SUBLANES, n), F32),
        grid=(depth, n // tn),
        in_specs=[
            pl.BlockSpec((SUBLANES, d), lambda l, j: (0, 0)),
            pl.BlockSpec((None, d, tn), lambda l, j: (l, 0, j)),
            pl.BlockSpec((None, 1, tn), lambda l, j: (l, 0, j)),
        ],
        out_specs=pl.BlockSpec((None, SUBLANES, tn), lambda l, j: (l, 0, j)),
        compiler_params=_cparams(("parallel", "parallel")),
        name="adaln_mod",
    )(c8, ada_w, ada_b.reshape(depth, 1, n))
    return out[:, :b_].reshape(depth, b_, 9, d)


def _ffn_kernel(x_ref, mod_ref, g_ref, wg_ref, wu_ref, wd_ref, fin_ref, o_ref, h_scr, acc_scr, *, sub, n_f, final):
    f = pl.program_id(1)

    @pl.when(f == 0)
    def _():
        h = _rms_mod(x_ref[...], g_ref[sub:sub + 1, :], mod_ref[3 * sub:3 * sub + 1, :],
                     mod_ref[3 * sub + 1:3 * sub + 2, :])
        h_scr[...] = h.astype(BF16)
        acc_scr[...] = jnp.zeros_like(acc_scr)

    h = h_scr[...]
    gate = jnp.dot(h, wg_ref[...], preferred_element_type=F32)
    up = jnp.dot(h, wu_ref[...], preferred_element_type=F32)
    act = (_silu(gate) * up).astype(BF16)
    acc_scr[...] += jnp.dot(act, wd_ref[...], preferred_element_type=F32)

    @pl.when(f == n_f - 1)
    def _():
        y = x_ref[...] + 0.5 * mod_ref[3 * sub + 2:3 * sub + 3, :] * acc_scr[...]
        if final:
            ms = jnp.mean(y * y, axis=-1, keepdims=True)
            y = y * lax.rsqrt(ms + NORM_EPS) * fin_ref[...]
        o_ref[...] = y


def _ffn(x, mod_l, norm_g_l, wg, wu, wd, final_g, *, sub, rows_per_batch, final):
    m, d = x.shape
    dff = wg.shape[-1]
    tm, tf = 512, 512
    n_f = dff // tf
    bpb = rows_per_batch // tm
    return pl.pallas_call(
        functools.partial(_ffn_kernel, sub=sub, n_f=n_f, final=final),
        out_shape=jax.ShapeDtypeStruct((m, d), F32),
        grid=(m // tm, n_f),
        in_specs=[
            pl.BlockSpec((tm, d), lambda i, f: (i, 0)),
            pl.BlockSpec((None, 9, d), lambda i, f: (i // bpb, 0, 0)),
            pl.BlockSpec((3, d), lambda i, f: (0, 0)),
            pl.BlockSpec((d, tf), lambda i, f: (0, f)),
            pl.BlockSpec((d, tf), lambda i, f: (0, f)),
            pl.BlockSpec((tf, d), lambda i, f: (f, 0)),
            pl.BlockSpec((1, d), lambda i, f: (0, 0)),
        ],
        out_specs=pl.BlockSpec((tm, d), lambda i, f: (i, 0)),
        scratch_shapes=[pltpu.VMEM((tm, d), BF16), pltpu.VMEM((tm, d), F32)],
        compiler_params=_cparams(("parallel", "arbitrary")),
        name="ffn_swiglu",
    )(x, mod_l, norm_g_l, wg, wu, wd, final_g)


def _inproj_kernel(x_ref, mod_ref, g_ref, w_ref, o_ref, h_scr, *, sub):
    @pl.when(pl.program_id(1) == 0)
    def _():
        h = _rms_mod(x_ref[...], g_ref[sub:sub + 1, :], mod_ref[3 * sub:3 * sub + 1, :],
                     mod_ref[3 * sub + 1:3 * sub + 2, :])
        h_scr[...] = h.astype(BF16)

    o_ref[...] = jnp.dot(h_scr[...], w_ref[...], preferred_element_type=F32)


def _inproj(x, mod_l, norm_g_l, w_pad, *, sub, rows_per_batch):
    m, d = x.shape
    n = w_pad.shape[-1]
    tm, tn = 1024, 512
    bpb = rows_per_batch // tm
    return pl.pallas_call(
        functools.partial(_inproj_kernel, sub=sub),
        out_shape=jax.ShapeDtypeStruct((m, n), F32),
        grid=(m // tm, n // tn),
        in_specs=[
            pl.BlockSpec((tm, d), lambda i, j: (i, 0)),
            pl.BlockSpec((None, 9, d), lambda i, j: (i // bpb, 0, 0)),
            pl.BlockSpec((3, d), lambda i, j: (0, 0)),
            pl.BlockSpec((d, tn), lambda i, j: (0, j)),
        ],
        out_specs=pl.BlockSpec((tm, tn), lambda i, j: (i, j)),
        scratch_shapes=[pltpu.VMEM((tm, d), BF16)],
        compiler_params=_cparams(("parallel", "arbitrary")),
        name="in_proj",
    )(x, mod_l, norm_g_l, w_pad)


def _outproj_kernel(x_ref, mod_ref, yn_ref, ym_ref, yr_ref, w_ref, o_ref, *, sub):
    acc = jnp.dot(yn_ref[...].astype(BF16), w_ref[0:NSA_W, :], preferred_element_type=F32)
    acc += jnp.dot(ym_ref[...].astype(BF16), w_ref[NSA_W:NSA_W + ML_W, :], preferred_element_type=F32)
    acc += jnp.dot(yr_ref[...].astype(BF16), w_ref[NSA_W + ML_W:, :], preferred_element_type=F32)
    o_ref[...] = x_ref[...] + mod_ref[3 * sub + 2:3 * sub + 3, :] * acc


def _outproj(x, mod_l, y_nsa, y_ml, y_rw, w_out, *, sub, rows_per_batch):
    m, d = x.shape
    tm = 256
    bpb = rows_per_batch // tm
    return pl.pallas_call(
        functools.partial(_outproj_kernel, sub=sub),
        out_shape=jax.ShapeDtypeStruct((m, d), F32),
        grid=(m // tm,),
        in_specs=[
            pl.BlockSpec((tm, d), lambda i: (i, 0)),
            pl.BlockSpec((None, 9, d), lambda i: (i // bpb, 0, 0)),
            pl.BlockSpec((tm, NSA_W), lambda i: (i, 0)),
            pl.BlockSpec((tm, ML_W), lambda i: (i, 0)),
            pl.BlockSpec((tm, RW_W), lambda i: (i, 0)),
            pl.BlockSpec((d, d), lambda i: (0, 0)),
        ],
        out_specs=pl.BlockSpec((tm, d), lambda i: (i, 0)),
        compiler_params=_cparams(("parallel",)),
        name="out_proj",
    )(x, mod_l, y_nsa, y_ml, y_rw, w_out)


ML_CHUNK = 256


def _shift_rows(x, tail, s, row):
    xs = pltpu.roll(x, s, 0)
    ts = pltpu.roll(tail, s, 0)
    top = jnp.where(row[:SUBLANES] < s, ts, xs[:SUBLANES])
    return jnp.concatenate([top, xs[SUBLANES:]], axis=0)


def _mlstm_kernel(zq_ref, zk_ref, zv_ref, zo_ref, zg_ref, gt_ref, cw_ref, cb_ref, gbc_ref, gbr_ref, ng_ref,
                  o_ref, c_scr, n_scr, m_scr, tq_scr, tk_scr):
    L = zq_ref.shape[0]
    hd = ML_HEAD_DIM

    @pl.when(pl.program_id(1) == 0)
    def _():
        c_scr[...] = jnp.zeros_like(c_scr)
        n_scr[...] = jnp.zeros_like(n_scr)
        m_scr[...] = jnp.zeros_like(m_scr)
        tq_scr[...] = jnp.zeros_like(tq_scr)
        tk_scr[...] = jnp.zeros_like(tk_scr)

    row = lax.broadcasted_iota(jnp.int32, (L, 1), 0)

    def conv(x_ref, tail_scr, w, b):
        x = x_ref[...]
        tail = tail_scr[...]
        acc = x * w[ML_CONV - 1:ML_CONV, :] + b
        for s in range(1, ML_CONV):
            acc = acc + _shift_rows(x, tail, s, row) * w[ML_CONV - 1 - s:ML_CONV - s, :]
        tail_scr[...] = x[L - SUBLANES:, :]
        return _silu(acc)

    cw = cw_ref[...]
    cb = cb_ref[...]
    q_all = conv(zq_ref, tq_scr, cw[:, :ML_W], cb[:, :ML_W])
    k_all = conv(zk_ref, tk_scr, cw[:, ML_W:], cb[:, ML_W:]) * (hd ** -0.5)

    g_col = zg_ref[...] + gbc_ref[...]
    g_row = gt_ref[...] + gbr_ref[...]
    ti = lax.broadcasted_iota(jnp.int32, (L, L), 0)
    si = lax.broadcasted_iota(jnp.int32, (L, L), 1)
    causal = si <= ti
    tril = causal.astype(F32)
    triu = (ti <= si).astype(F32)
    bcum_col = jnp.dot(tril, _log_sigmoid(g_col), precision=HIGHEST, preferred_element_type=F32)
    bcum_row = jnp.dot(_log_sigmoid(g_row), triu, precision=HIGHEST, preferred_element_type=F32)

    for h in range(ML_HEADS):
        sl = slice(h * hd, (h + 1) * hd)
        q = q_all[:, sl]
        k = k_all[:, sl]
        v = zv_ref[:, sl]
        i_col = g_col[:, h:h + 1]
        b_col = bcum_col[:, ML_HEADS + h:ML_HEADS + h + 1]
        i_row = g_row[h:h + 1, :]
        b_row = bcum_row[ML_HEADS + h:ML_HEADS + h + 1, :]
        m_prev = m_scr[h][:, 0:1]
        c_prev = c_scr[h]
        n_prev = n_scr[h]

        log_d = jnp.where(causal, b_col + (i_row - b_row), MASK_NEG)
        m_inter = b_col + m_prev
        m_t = jnp.maximum(jnp.max(log_d, axis=-1, keepdims=True), m_inter)
        dmat = jnp.exp(log_d - m_t)
        qb = q.astype(BF16)
        kb = k.astype(BF16)
        vb = v.astype(BF16)
        s = lax.dot_general(qb, kb, (((1,), (1,)), ((), ())), preferred_element_type=F32) * dmat
        w_inter = jnp.exp(m_inter - m_t)
        num = jnp.dot(s.astype(BF16), vb, preferred_element_type=F32)
        num += w_inter * jnp.dot(qb, c_prev.astype(BF16), preferred_element_type=F32)
        den = jnp.sum(s, axis=-1, keepdims=True) + w_inter * jnp.sum(q * n_prev, axis=-1, keepdims=True)
        hh = num / jnp.maximum(jnp.abs(den), jnp.exp(-m_t))

        g_last = b_row[:, L - 1:L]
        lw_row = g_last - b_row + i_row
        m_new = jnp.maximum(g_last + m_prev, jnp.max(lw_row, axis=-1, keepdims=True))
        wk_col = jnp.exp(g_last - b_col + i_col - m_new)
        dec = jnp.exp(g_last + m_prev - m_new)
        kw = k * wk_col
        c_scr[h] = dec * c_prev + lax.dot_general(kw.astype(BF16), vb, (((0,), (0,)), ((), ())),
                                                  preferred_element_type=F32)
        n_scr[h] = dec * n_prev + jnp.sum(kw, axis=0, keepdims=True)
        m_scr[h] = jnp.broadcast_to(m_new, (1, LANES))

        hh = hh * _sigmoid(zo_ref[:, sl])
        mu = jnp.mean(hh, axis=-1, keepdims=True)
        var = jnp.mean(jnp.square(hh - mu), axis=-1, keepdims=True)
        o_ref[:, sl] = (hh - mu) * lax.rsqrt(var + NORM_EPS) * ng_ref[:, sl]


def _mlstm(z_pad, gates_t, conv_w, conv_b, gate_b, norm_g, *, batch, seq):
    L = min(ML_CHUNK, seq)
    nc = seq // L
    m = batch * seq
    gb = gate_b.reshape(2 * ML_HEADS)
    gb_col = jnp.zeros((1, LANES), F32).at[0, :2 * ML_HEADS].set(gb)
    gb_row = jnp.broadcast_to(gb[:, None], (2 * ML_HEADS, L))
    cb = lambda j: pl.BlockSpec((L, ML_W), lambda b, c: (b * nc + c, _Z_ML // ML_W + j))
    return pl.pallas_call(
        _mlstm_kernel,
        out_shape=jax.ShapeDtypeStruct((m, ML_W), F32),
        grid=(batch, nc),
        in_specs=[
            cb(0), cb(1), cb(2), cb(3),
            pl.BlockSpec((L, LANES), lambda b, c: (b * nc + c, _Z_ML_GATE // LANES)),
            pl.BlockSpec((2 * ML_HEADS, L), lambda b, c: (0, b * nc + c)),
            pl.BlockSpec((ML_CONV, 2 * ML_W), lambda b, c: (0, 0)),
            pl.BlockSpec((1, 2 * ML_W), lambda b, c: (0, 0)),
            pl.BlockSpec((1, LANES), lambda b, c: (0, 0)),
            pl.BlockSpec((2 * ML_HEADS, L), lambda b, c: (0, 0)),
            pl.BlockSpec((1, ML_W), lambda b, c: (0, 0)),
        ],
        out_specs=pl.BlockSpec((L, ML_W), lambda b, c: (b * nc + c, 0)),
        scratch_shapes=[
            pltpu.VMEM((ML_HEADS, ML_HEAD_DIM, ML_HEAD_DIM), F32),
            pltpu.VMEM((ML_HEADS, 1, ML_HEAD_DIM), F32),
            pltpu.VMEM((ML_HEADS, 1, LANES), F32),
            pltpu.VMEM((SUBLANES, ML_W), F32),
            pltpu.VMEM((SUBLANES, ML_W), F32),
        ],
        compiler_params=_cparams(("parallel", "arbitrary")),
        name="mlstm",
    )(z_pad, z_pad, z_pad, z_pad, z_pad, gates_t, conv_w, conv_b.reshape(1, -1), gb_col, gb_row,
      norm_g.reshape(1, -1))


RW_CHUNK = 64
RW_PAIRS = RW_HEADS // 2


_NN = (((1,), (0,)), ((), ()))
_NT = (((1,), (1,)), ((), ()))
_TN = (((0,), (0,)), ((), ()))


def _split(x, parts):
    if isinstance(x, tuple):
        return x
    out = []
    for _ in range(parts - 1):
        h = x.astype(BF16)
        out.append(h)
        x = x - h.astype(F32)
    out.append(x.astype(BF16))
    return tuple(out)


def _dot_split(a, b, dims=_NN, a_parts=2, b_parts=2, order=1):
    a = _split(a, a_parts)
    b = _split(b, b_parts)
    pairs = [(ai, bj) for i, ai in enumerate(a) for j, bj in enumerate(b) if i + j <= order]
    a_cat = jnp.concatenate([p[0] for p in pairs], axis=dims[0][0][0])
    b_cat = jnp.concatenate([p[1] for p in pairs], axis=dims[0][1][0])
    return lax.dot_general(a_cat, b_cat, dims, preferred_element_type=F32)


def _rwkv_kernel(zr_ref, zk_ref, zv_ref, zwa_ref, zgd_ref, mur_ref, muk_ref, muv_ref, muwa_ref, mugd_ref,
                 w0_ref, wup_ref, a0_ref, aup_ref, gup_ref, kk_ref, ka_ref, rk_ref, ln_ref, ones_ref,
                 o_ref, s_scr, tr_scr, tk_scr, tv_scr, twa_scr, tgd_scr):
    C = zr_ref.shape[0]
    hd = RW_HEAD_DIM

    @pl.when(pl.program_id(1) == 0)
    def _():
        s_scr[...] = jnp.zeros_like(s_scr)
        tr_scr[...] = jnp.zeros_like(tr_scr)
        tk_scr[...] = jnp.zeros_like(tk_scr)
        tv_scr[...] = jnp.zeros_like(tv_scr)
        twa_scr[...] = jnp.zeros_like(twa_scr)
        tgd_scr[...] = jnp.zeros_like(tgd_scr)

    row = lax.broadcasted_iota(jnp.int32, (C, 1), 0)

    def shift(x_ref, t_scr, mu_ref):
        x = x_ref[...]
        xp = jnp.where(row == 0, t_scr[...], pltpu.roll(x, 1, 0))
        t_scr[...] = x[C - 1:C, :]
        return x + mu_ref[...] * (xp - x)

    zr = shift(zr_ref, tr_scr, mur_ref)
    zk = shift(zk_ref, tk_scr, muk_ref)
    zv = shift(zv_ref, tv_scr, muv_ref)
    zwa = shift(zwa_ref, twa_scr, muwa_ref)
    zgd = shift(zgd_ref, tgd_scr, mugd_ref)

    ones = (ones_ref[...],)

    def seg_sum(x):
        return _dot_split(x, ones, a_parts=3, order=2)

    u = -(w0_ref[...] + jnp.dot(jnp.tanh(zwa).astype(BF16), wup_ref[...], preferred_element_type=F32))
    softplus = jnp.maximum(u, 0.0) + jnp.log(1.0 + jnp.exp(-jnp.abs(u)))
    logw = -jnp.exp(-softplus - 0.5)
    a = _sigmoid(a0_ref[...] + jnp.dot(zwa.astype(BF16), aup_ref[...], preferred_element_type=F32))
    g = jnp.dot(_sigmoid(zgd).astype(BF16), gup_ref[...], preferred_element_type=F32)
    kk = zk * kk_ref[...]
    k2 = zk * (1.0 + (a - 1.0) * ka_ref[...])
    sums = seg_sum(jnp.concatenate([kk * kk, zr * k2 * rk_ref[...]], axis=0))
    kk = kk / jnp.maximum(jnp.sqrt(sums[:C]), 1e-12)
    bonus_w = sums[C:]
    bv = kk * a

    tc = lax.broadcasted_iota(jnp.int32, (C, C), 0)
    sc = lax.broadcasted_iota(jnp.int32, (C, C), 1)
    tril = ((sc <= tc).astype(BF16),)
    lc = _dot_split(tril, logw, b_parts=3, order=2)
    l_last = lc[C - 1:C, :]
    e_neg = jnp.exp(-lc)
    e_end = jnp.exp(l_last - lc)
    at = -kk * jnp.exp(lc - logw)
    rt = zr * jnp.exp(lc)
    bt = bv * e_neg
    kt = k2 * e_neg
    bte = bv * e_end
    kte = k2 * e_end
    wc = jnp.exp(l_last)

    lane = lax.broadcasted_iota(jnp.int32, (1, LANES), 1)
    head_a = lane < hd
    t2 = lax.broadcasted_iota(jnp.int32, (2 * C, 2 * C), 0)
    s2 = lax.broadcasted_iota(jnp.int32, (2 * C, 2 * C), 1)
    same = (t2 < C) == (s2 < C)
    strict = same & (s2 < t2)
    incl = same & (s2 <= t2)
    eye = (s2 == t2).astype(F32)

    def stack_masked(x):
        return jnp.concatenate([jnp.where(head_a, x, 0.0), jnp.where(head_a, 0.0, x)], axis=0)

    def stack(x):
        return jnp.concatenate([x, x], axis=0)

    n2 = 2 * C
    pairs = range(RW_PAIRS)
    sls = [slice(p * LANES, (p + 1) * LANES) for p in pairs]
    v_s = [stack_masked(zv[:, sl]) for sl in sls]
    s0 = [s_scr[p] for p in pairs]
    gram = [_dot_split(jnp.concatenate([stack_masked(at[:, sl]), stack_masked(rt[:, sl])], axis=0),
                       jnp.concatenate([stack(bt[:, sl]), stack(kt[:, sl]), s0[p]], axis=0), _NT)
            for p, sl in enumerate(sls)]
    a_ab = [jnp.where(strict, gm[:n2, :n2], 0.0) for gm in gram]
    x = [_dot_split(n, n) for n in a_ab]
    tinv = [eye + n for n in a_ab]
    n_sq = int(math.log2(C)) - 1
    for i in range(1, n_sq):
        both = [_dot_split(jnp.concatenate([x[p], tinv[p]], axis=0), x[p]) for p in pairs]
        x = [bo[:n2] for bo in both]
        tinv = [tinv[p] + both[p][n2:] for p in pairs]
    tinv = [tinv[p] + _dot_split(tinv[p], x[p]) for p in pairs]
    rhs = [gram[p][:n2, 2 * n2:] + _dot_split(jnp.where(strict, gram[p][:n2, n2:2 * n2], 0.0), v_s[p])
           for p in pairs]
    uu = [_dot_split(tinv[p], rhs[p]) for p in pairs]
    uv = [_split(jnp.concatenate([uu[p], v_s[p]], axis=0), 2) for p in pairs]
    yy = [gram[p][n2:, 2 * n2:]
          + _dot_split(jnp.concatenate([jnp.where(incl, gram[p][n2:, :n2], 0.0),
                                        jnp.where(incl, gram[p][n2:, n2:2 * n2], 0.0)], axis=1), uv[p])
          for p in pairs]
    ys = [o[:C] + o[C:] for o in yy]
    for p, sl in enumerate(sls):
        ends = jnp.concatenate([stack_masked(bte[:, sl]), stack_masked(kte[:, sl])], axis=0)
        s_scr[p] = s0[p] * wc[:, sl] + _dot_split(uv[p], ends, _TN)

    y = jnp.concatenate(ys, axis=1)
    inv_hd = 1.0 / hd
    mean = seg_sum(y) * inv_hd
    yc = y - mean
    var = seg_sum(yc * yc) * inv_hd
    yn = yc * lax.rsqrt(var + RW_LN_EPS) * ln_ref[0:1, :] + ln_ref[1:2, :]
    o_ref[...] = (yn + bonus_w * zv) * g


def _rwkv(z_pad, mu, w0, w_up, a0, a_up, g_up, k_k, k_a, r_k, ln, *, batch, seq):
    C = RW_CHUNK
    nc = seq // C
    m = batch * seq
    w = RW_W
    row = lambda v: v.reshape(1, -1).astype(F32)
    wup_pad = jnp.zeros((LANES, w), F32).at[:RW_DECAY_RANK].set(w_up).astype(BF16)
    aup_pad = jnp.zeros((LANES, w), F32).at[RW_DECAY_RANK:].set(a_up).astype(BF16)
    head_of = np.arange(w) // RW_HEAD_DIM
    ones = jnp.asarray((head_of[:, None] == head_of[None, :]).astype(np.float32)).astype(BF16)
    zspec = lambda width, off: pl.BlockSpec((C, width), lambda b, c: (b * nc + c, off // width))
    full = lambda shape: pl.BlockSpec(shape, lambda b, c: tuple(0 for _ in shape))
    return pl.pallas_call(
        _rwkv_kernel,
        out_shape=jax.ShapeDtypeStruct((m, w), F32),
        grid=(batch, nc),
        in_specs=[
            zspec(w, _Z_RW), zspec(w, _Z_RW + w), zspec(w, _Z_RW + 2 * w),
            zspec(LANES, _Z_RW_WA), zspec(LANES, _Z_RW_GD),
            full((1, w)), full((1, w)), full((1, w)), full((1, LANES)), full((1, LANES)),
            full((1, w)), full((LANES, w)), full((1, w)), full((LANES, w)), full((LANES, w)),
            full((1, w)), full((1, w)), full((1, w)), full((2, w)), full((w, w)),
        ],
        out_specs=pl.BlockSpec((C, w), lambda b, c: (b * nc + c, 0)),
        scratch_shapes=[
            pltpu.VMEM((RW_PAIRS, LANES, LANES), F32),
            pltpu.VMEM((1, w), F32), pltpu.VMEM((1, w), F32), pltpu.VMEM((1, w), F32),
            pltpu.VMEM((1, LANES), F32), pltpu.VMEM((1, LANES), F32),
        ],
        compiler_params=_cparams(("parallel", "arbitrary")),
        name="rwkv7",
    )(z_pad, z_pad, z_pad, z_pad, z_pad,
      row(mu[:w]), row(mu[w:2 * w]), row(mu[2 * w:3 * w]), row(mu[3 * w:3 * w + LANES]), row(mu[3 * w + LANES:]),
      row(w0), wup_pad, row(a0), aup_pad, g_up.astype(BF16), row(k_k), row(k_a), row(r_k), ln.astype(F32), ones)


CMP_PER_Q = Q_BLOCK // CMP_STRIDE
CMP_PER_SEL = SEL_BLOCK // CMP_STRIDE
QG = NSA_GROUP * Q_BLOCK
FAR_BUCKET = REL_BUCKETS - 1
LOG2E = math.log2(math.e)
NSA_TILE = 2 * SEL_BLOCK
NSA_CHUNK = 4 * NSA_TILE
NSA_V_ROWS = NSA_HEAD_DIM + 16


def _bucket_table(n_max):
    n = np.arange(n_max)
    max_exact = REL_BUCKETS // 2
    def large(dt):
        v = np.log(np.maximum(n, max_exact).astype(dt) / dt(max_exact)) / dt(math.log(REL_MAX_DIST / max_exact))
        return max_exact + (v * dt(REL_BUCKETS - max_exact)).astype(np.int32)
    l32, l64 = large(np.float32), large(np.float64)
    assert np.array_equal(np.minimum(l32, FAR_BUCKET), np.minimum(l64, FAR_BUCKET))
    return np.where(n < max_exact, n, np.minimum(l64, FAR_BUCKET)).astype(np.int32)


_FAR_DIST = int(np.argmax(_bucket_table(4096) == FAR_BUCKET))
assert np.all(_bucket_table(4096)[_FAR_DIST:] == FAR_BUCKET) and _FAR_DIST <= Q_BLOCK - CMP_STRIDE + 1


def _rel_bias_tile(rel_bias, dist, valid):
    table = _bucket_table(int(dist.max()) + 2)
    idx = table[np.clip(dist, 0, None)]
    tile = rel_bias.astype(F32)[:, idx] * LOG2E
    tile = jnp.where(jnp.asarray(valid)[None], tile, MASK_NEG)
    rows = dist.shape[0]
    tile = tile.reshape(NSA_KV_HEADS, NSA_GROUP, rows, Q_BLOCK).transpose(0, 2, 1, 3)
    return tile.reshape(NSA_KV_HEADS, rows, QG)


def _compress_kernel(r_ref, pos_ref, w1_ref, w2_ref, o_ref):
    nc = r_ref.shape[0]
    half = CMP_STRIDE * NSA_HEAD_DIM
    r = r_ref[...]
    w1 = w1_ref[...].astype(BF16)
    p_top = pos_ref[0:1, :]
    p_bot = pos_ref[1:2, :]
    g1 = jnp.dot((r + p_top).astype(BF16), w1[:half], preferred_element_type=F32)
    g2 = jnp.dot((r + p_bot).astype(BF16), w1[half:], preferred_element_type=F32)
    tail = jnp.dot(jnp.broadcast_to(p_bot, (SUBLANES, half)).astype(BF16), w1[half:],
                   preferred_element_type=F32)[0:1]
    row = lax.broadcasted_iota(jnp.int32, (nc, 1), 0)
    g2_next = jnp.where(row == nc - 1, tail, pltpu.roll(g2, nc - 1, 0))
    hid = _silu(g1 + g2_next)
    o_ref[...] = jnp.dot(hid.astype(BF16), w2_ref[...].astype(BF16), preferred_element_type=F32)


def _compress(r_kv, cmp_pos, cmp_w1, cmp_w2):
    two, bh, nc, width = r_kv.shape
    pos = cmp_pos.reshape(2, 2, width)
    return pl.pallas_call(
        _compress_kernel,
        out_shape=jax.ShapeDtypeStruct((2, bh, nc, NSA_HEAD_DIM), F32),
        grid=(2, bh),
        in_specs=[
            pl.BlockSpec((None, None, nc, width), lambda s, i: (s, i, 0, 0)),
            pl.BlockSpec((None, 2, width), lambda s, i: (s, 0, 0)),
            pl.BlockSpec((None, 2 * width, CMP_HIDDEN), lambda s, i: (s, 0, 0)),
            pl.BlockSpec((None, CMP_HIDDEN, NSA_HEAD_DIM), lambda s, i: (s, 0, 0)),
        ],
        out_specs=pl.BlockSpec((None, None, nc, NSA_HEAD_DIM), lambda s, i: (s, i, 0, 0)),
        compiler_params=_cparams(("parallel", "parallel")),
        name="nsa_compress",
    )(r_kv, pos, cmp_w1, cmp_w2)


def _nsa_cmp_kernel(qt_ref, kc_ref, vct_ref, far_ref, delta_ref, ocmp_ref, sel_ref, flag_ref, s_scr, imp_scr,
                    *, n_sel):
    qb = pl.program_id(2)
    nc = kc_ref.shape[0]
    nb = nc // CMP_PER_SEL
    pad = SUBLANES

    s = jnp.dot(kc_ref[...].astype(BF16), qt_ref[...], preferred_element_type=F32) + far_ref[...]
    s_scr[0:pad, :] = jnp.zeros((pad, QG), F32)
    s_scr[pad:, :] = s
    win = pl.ds(pl.multiple_of(CMP_PER_Q * qb, SUBLANES), 2 * CMP_PER_Q)
    s_scr[win, :] = s_scr[win, :] + delta_ref[...]
    s = s_scr[pad:, :]

    n_idx = lax.broadcasted_iota(jnp.int32, (nc, 1), 0)
    lane = lax.broadcasted_iota(jnp.int32, (1, QG), 1)
    t_q = qb * Q_BLOCK + (lane & (Q_BLOCK - 1))
    valid = n_idx * CMP_STRIDE + (CMP_LEN - 1) <= t_q
    m = jnp.max(jnp.where(valid, s, MASK_NEG), axis=0, keepdims=True)
    p = jnp.where(valid, jnp.exp2(s - m), 0.0)
    l = jnp.sum(p, axis=0, keepdims=True)
    pn = p * jnp.where(l > 0.0, 1.0 / l, 0.0)
    ocmp_ref[...] = jnp.dot(vct_ref[...].astype(BF16), pn.astype(BF16), preferred_element_type=F32)

    imp = pn[:, 0:Q_BLOCK]
    for g in range(1, NSA_GROUP):
        imp = imp + pn[:, g * Q_BLOCK:(g + 1) * Q_BLOCK]
    imp_scr[...] = imp
    parts = [imp_scr[pl.ds(c, nb, stride=CMP_PER_SEL), :] for c in range(CMP_PER_SEL)]
    score = parts[0]
    for c in range(1, CMP_PER_SEL):
        score = score + parts[c]
    blk = lax.broadcasted_iota(jnp.int32, (nb, 1), 0)
    score = score + jnp.where(blk == 0, 0.0, pltpu.roll(parts[-1], 1, 0))

    t = qb * Q_BLOCK + lax.broadcasted_iota(jnp.int32, (1, Q_BLOCK), 1)
    cur = t // SEL_BLOCK
    forced = (blk == 0) | (blk == cur) | (blk == cur - 1)
    ok = blk * SEL_BLOCK <= t
    score = jnp.where(forced, -MASK_NEG, jnp.where(ok, score, MASK_NEG))
    sel = jnp.zeros((nb, Q_BLOCK), F32)
    for _ in range(n_sel):
        top = jnp.max(score, axis=0, keepdims=True)
        first = jnp.min(jnp.where(score == top, blk, nb), axis=0, keepdims=True)
        hit = blk == first
        sel = jnp.where(hit, 1.0, sel)
        score = jnp.where(hit, -jnp.inf, score)
    sel = jnp.where(ok, sel, 0.0)
    sel_ref[...] = sel
    cnt = lax.dot_general(jnp.ones((SUBLANES, Q_BLOCK), F32), sel, (((1,), (1,)), ((), ())),
                          preferred_element_type=F32)
    flag_ref[...] = (cnt[0:1, :] > 0.0).astype(jnp.int32)


def _nsa_attn_kernel(flag_ref, qt_ref, sel_ref, ocmp_ref, gate_ref, ks_ref, vst_ref, kw_ref, vwt_ref,
                     qaug_ref, bdiag_ref, bwin_ref, o_ref, q_scr, mb_scr, m_scr, acc_scr, sa_scr, sb_scr, sn_scr,
                     sw_scr, p_scr, lst_scr):
    qb = pl.program_id(2)
    hd = NSA_HEAD_DIM
    per_tile = NSA_TILE // SEL_BLOCK
    blocks_per_chunk = NSA_CHUNK // SEL_BLOCK
    tiles_per_chunk = NSA_CHUNK // NSA_TILE

    q_scr[0:hd, :] = qt_ref[...]
    q_scr[hd:, :] = qaug_ref[...]
    mb = (sel_ref[...] - 1.0) * (-MASK_NEG)
    mb_scr[...] = jnp.concatenate([mb] * NSA_GROUP, axis=1)

    def reset():
        m_scr[...] = jnp.full_like(m_scr, MASK_NEG)
        acc_scr[...] = jnp.zeros_like(acc_scr)

    def consume(s_ref, rows, vt):
        blocks = [slice(j * SEL_BLOCK, (j + 1) * SEL_BLOCK) for j in range(len(rows))]
        m_new = m_scr[...]
        m_old = m_new
        for bl, r in zip(blocks, rows):
            m_new = jnp.maximum(m_new, jnp.max(s_ref[bl, :], axis=0, keepdims=True) + r)
        seen = m_new > 0.5 * MASK_NEG
        for bl, r in zip(blocks, rows):
            offset = jnp.where(seen, m_new - r, -MASK_NEG)
            p_scr[bl, :] = jnp.exp2(s_ref[bl, :] - offset).astype(BF16)
        n = len(rows) * SEL_BLOCK
        acc_scr[...] = jnp.exp2(m_old - m_new) * acc_scr[...] + jnp.dot(vt, p_scr[0:n, :],
                                                                    preferred_element_type=F32)
        m_scr[...] = m_new

    def finish():
        return acc_scr[0:hd, :] * (1.0 / acc_scr[hd:hd + 1, :])

    def penalty(ok):
        return jnp.where(ok, 0.0, MASK_NEG)

    near_tiles = WINDOW // NSA_TILE + 1
    near_keys = near_tiles * NSA_TILE
    near_blocks = near_keys // SEL_BLOCK
    near_start = pl.multiple_of(qb * NSA_TILE, NSA_TILE)
    first_near_block = per_tile * (qb - (near_tiles - 1))
    n_far = jnp.maximum(qb - (near_tiles - 1), 0)
    far_blocks = per_tile * n_far
    n_chunks = (n_far + tiles_per_chunk - 1) // tiles_per_chunk

    lst_scr[0] = 0

    def build(c, cnt):
        active = 0
        for j in range(blocks_per_chunk):
            blk = blocks_per_chunk * c + j
            active = active + jnp.where(blk < far_blocks, flag_ref[0, blk], 0)
        lst_scr[cnt] = c
        return cnt + (active > 0).astype(jnp.int32)

    cnt = lax.fori_loop(0, n_chunks, build, 0)

    def chunk_rows(c):
        return pl.multiple_of(c * NSA_CHUNK + WINDOW, NSA_TILE)

    def qk(c, buf):
        buf[...] = jnp.dot(ks_ref[pl.ds(chunk_rows(c), NSA_CHUNK), :], q_scr[...], preferred_element_type=F32)

    def far_consume(c, buf, live):
        blk0 = pl.multiple_of(c * blocks_per_chunk, blocks_per_chunk)
        blk = blk0 + lax.broadcasted_iota(jnp.int32, (blocks_per_chunk, 1), 0)
        rows = mb_scr[pl.ds(blk0, blocks_per_chunk), :] + penalty((blk < far_blocks) & live)
        consume(buf, [rows[j:j + 1] for j in range(blocks_per_chunk)],
                vst_ref[:, pl.ds(chunk_rows(c), NSA_CHUNK)])

    reset()
    qk(lst_scr[0], sa_scr)

    def pair_body(j, carry):
        last = cnt - 1
        c0 = lst_scr[2 * j]
        c1 = lst_scr[jnp.minimum(2 * j + 1, last)]
        c2 = lst_scr[jnp.minimum(2 * j + 2, last)]
        qk(c1, sb_scr)
        far_consume(c0, sa_scr, True)
        qk(c2, sa_scr)
        far_consume(c1, sb_scr, 2 * j + 1 < cnt)
        return carry

    lax.fori_loop(0, (cnt + 1) // 2, pair_body, 0)

    n_plain = near_keys - 2 * NSA_TILE
    s = jnp.dot(ks_ref[pl.ds(near_start, near_keys), :], q_scr[...], preferred_element_type=F32)
    sn_scr[0:n_plain, :] = s[:n_plain]
    sn_scr[n_plain:, :] = s[n_plain:] + bdiag_ref[...]
    sw_scr[...] = (jnp.dot(kw_ref[pl.ds(near_start, near_keys), :], q_scr[...], preferred_element_type=F32)
                   + bwin_ref[...])
    rows = []
    for j in range(near_blocks):
        blk = first_near_block + j
        rows.append(mb_scr[pl.ds(jnp.maximum(blk, 0), 1), :] + penalty(blk >= 0))
    consume(sn_scr, rows, vst_ref[:, pl.ds(near_start, near_keys)])
    o_slc = finish()

    reset()
    consume(sw_scr, [penalty(first_near_block + j >= 0) for j in range(near_blocks)],
            vwt_ref[:, pl.ds(near_start, near_keys)])
    o_win = finish()

    sig = _sigmoid(gate_ref[...])

    def gate_row(branch):
        return jnp.concatenate([sig[g * 3 + branch:g * 3 + branch + 1, :] for g in range(NSA_GROUP)], axis=1)

    o_t = gate_row(0) * ocmp_ref[...] + gate_row(1) * o_slc + gate_row(2) * o_win
    stacked = jnp.concatenate([o_t[:, g * Q_BLOCK:(g + 1) * Q_BLOCK] for g in range(NSA_GROUP)], axis=0)
    o_ref[...] = stacked.T


def _nsa(z_pad, rel_bias, cmp_pos, cmp_w1, cmp_w2, *, batch, seq):
    kvh, grp, hd = NSA_KV_HEADS, NSA_GROUP, NSA_HEAD_DIM
    bh = batch * kvh
    nqb = seq // Q_BLOCK
    nc = seq // CMP_STRIDE
    nb = seq // SEL_BLOCK
    n_sel = min(N_SEL, nb)
    m = batch * seq

    zq = z_pad[:, _Z_NSA_Q:_Z_NSA_Q + NSA_W].reshape(batch, nqb, Q_BLOCK, kvh, grp, hd)
    qt = (zq * (hd ** -0.5 * LOG2E)).astype(BF16).transpose(0, 3, 1, 5, 4, 2).reshape(bh, nqb, hd, QG)
    kv = z_pad[:, _Z_NSA_KV:_Z_NSA_KV + 6 * NSA_KV_W].reshape(batch, seq, 6, kvh, hd)
    kv = kv.transpose(2, 0, 3, 1, 4).reshape(6, bh, seq, hd)
    r_kv = kv[0:2].reshape(2, bh, nc, CMP_STRIDE * hd)
    def k_aug(k, fold):
        extra = jnp.full((bh, seq, 2), fold, BF16)
        rows = jnp.concatenate([k.astype(BF16), extra, jnp.zeros((bh, seq, LANES - hd - 2), BF16)], axis=-1)
        return jnp.pad(rows, ((0, 0), (WINDOW, 0), (0, 0)))

    def vt_aug(v):
        cols = jnp.concatenate([v.astype(BF16).transpose(0, 2, 1), jnp.ones((bh, 1, seq), BF16),
                                jnp.zeros((bh, NSA_V_ROWS - hd - 1, seq), BF16)], axis=1)
        return jnp.pad(cols, ((0, 0), (0, 0), (WINDOW, 0)))

    ks = k_aug(kv[2], 1.0)
    vst = vt_aug(kv[3])
    kw = k_aug(kv[4], 0.0)
    vwt = vt_aug(kv[5])
    gates_t = z_pad[:, _Z_NSA_GT:_Z_NSA_GT + 3 * NSA_HEADS].reshape(batch, seq, kvh, grp * 3)
    gates_t = gates_t.transpose(0, 2, 3, 1).reshape(bh, grp * 3, seq)

    qi = np.arange(Q_BLOCK)[None, :]
    far = _rel_bias_tile(rel_bias, np.full((1, Q_BLOCK), _FAR_DIST), np.ones((1, Q_BLOCK), bool))
    mrow = np.arange(-CMP_PER_Q, CMP_PER_Q)[:, None]
    d_cmp = qi - CMP_STRIDE * mrow - (CMP_LEN - 1)
    delta = _rel_bias_tile(rel_bias, d_cmp, d_cmp >= 0)
    delta = jnp.where(jnp.asarray(np.tile(d_cmp >= 0, (1, grp)))[None], delta - far, 0.0)
    d_diag = qi + NSA_TILE - np.arange(2 * NSA_TILE)[:, None]
    bdiag = _rel_bias_tile(rel_bias, d_diag, d_diag >= 0) - far
    far_hi = far.astype(BF16)
    far_lo = (far - far_hi.astype(F32)).astype(BF16)
    q_aug = jnp.concatenate([far_hi, far_lo, jnp.zeros((kvh, LANES - hd - 2, QG), BF16)], axis=1)
    d_win = qi + WINDOW - np.arange(WINDOW + NSA_TILE)[:, None]
    bwin = _rel_bias_tile(rel_bias, d_win, (d_win >= 0) & (d_win < WINDOW))

    cmp_kv = _compress(r_kv, cmp_pos, cmp_w1, cmp_w2)
    k_cmp = cmp_kv[0]
    v_cmp_t = cmp_kv[1].transpose(0, 2, 1)

    g3 = (batch, kvh, nqb)
    bhq = lambda b, h, q: (b * kvh + h, q, 0, 0)
    bh0 = lambda b, h, q: (b * kvh + h, 0, 0)
    per_h = lambda b, h, q: (h, 0, 0)
    o_cmp, sel, flags = pl.pallas_call(
        functools.partial(_nsa_cmp_kernel, n_sel=n_sel),
        out_shape=(jax.ShapeDtypeStruct((bh, nqb, hd, QG), F32),
                   jax.ShapeDtypeStruct((bh, nqb, nb, Q_BLOCK), F32),
                   jax.ShapeDtypeStruct((bh, nqb, 1, nb), jnp.int32)),
        grid=g3,
        in_specs=[
            pl.BlockSpec((None, None, hd, QG), bhq),
            pl.BlockSpec((None, nc, hd), bh0),
            pl.BlockSpec((None, hd, nc), bh0),
            pl.BlockSpec((None, 1, QG), per_h),
            pl.BlockSpec((None, 2 * CMP_PER_Q, QG), per_h),
        ],
        out_specs=(pl.BlockSpec((None, None, hd, QG), bhq),
                   pl.BlockSpec((None, None, nb, Q_BLOCK), bhq),
                   pl.BlockSpec((None, None, 1, nb), bhq)),
        scratch_shapes=[pltpu.VMEM((nc + SUBLANES, QG), F32), pltpu.VMEM((nc, Q_BLOCK), F32)],
        compiler_params=_cparams(("parallel", "parallel", "arbitrary")),
        name="nsa_cmp_select",
    )(qt, k_cmp, v_cmp_t, far, delta)

    return pl.pallas_call(
        _nsa_attn_kernel,
        out_shape=jax.ShapeDtypeStruct((m, NSA_W), F32),
        grid=g3,
        in_specs=[
            pl.BlockSpec((None, None, 1, nb), bhq, memory_space=pltpu.SMEM),
            pl.BlockSpec((None, None, hd, QG), bhq),
            pl.BlockSpec((None, None, nb, Q_BLOCK), bhq),
            pl.BlockSpec((None, None, hd, QG), bhq),
            pl.BlockSpec((None, grp * 3, Q_BLOCK), lambda b, h, q: (b * kvh + h, 0, q)),
            pl.BlockSpec((None, WINDOW + seq, LANES), bh0),
            pl.BlockSpec((None, NSA_V_ROWS, WINDOW + seq), bh0),
            pl.BlockSpec((None, WINDOW + seq, LANES), bh0),
            pl.BlockSpec((None, NSA_V_ROWS, WINDOW + seq), bh0),
            pl.BlockSpec((None, LANES - hd, QG), per_h),
            pl.BlockSpec((None, 2 * NSA_TILE, QG), per_h),
            pl.BlockSpec((None, WINDOW + NSA_TILE, QG), per_h),
        ],
        out_specs=pl.BlockSpec((Q_BLOCK, grp * hd), lambda b, h, q: (b * nqb + q, h)),
        scratch_shapes=[pltpu.VMEM((LANES, QG), BF16), pltpu.VMEM((nb, QG), F32), pltpu.VMEM((1, QG), F32),
                        pltpu.VMEM((NSA_V_ROWS, QG), F32), pltpu.VMEM((NSA_CHUNK, QG), F32),
                        pltpu.VMEM((NSA_CHUNK, QG), F32), pltpu.VMEM((WINDOW + NSA_TILE, QG), F32),
                        pltpu.VMEM((WINDOW + NSA_TILE, QG), F32), pltpu.VMEM((WINDOW + NSA_TILE, QG), BF16),
                        pltpu.SMEM((seq // NSA_CHUNK + 1,), jnp.int32)],
        compiler_params=_cparams(("parallel", "parallel", "arbitrary")),
        name="nsa_attention",
    )(flags, qt, sel, o_cmp, gates_t, ks, vst, kw, vwt, q_aug, bdiag, bwin)


def kernel(x, c, rel_bias, final_norm, ada_w, ada_b, norm_g, ffn_w_gate, ffn_w_up, ffn_w_down, w_in, w_out,
           cmp_pos, cmp_w1, cmp_w2, ml_conv_w, ml_conv_b, ml_gate_b, ml_norm, rw_mu, rw_w0, rw_w_up, rw_a0,
           rw_a_up, rw_g_up, rw_k_k, rw_k_a, rw_r_k, rw_ln):
    b_, t_, d = x.shape
    m = b_ * t_
    mod = _adaln_mod(c, ada_w, ada_b)
    src = _z_column_map()
    col_ok = jnp.asarray(src >= 0)
    col_src = np.maximum(src, 0)
    fin = final_norm.reshape(1, d)
    xf = x.reshape(m, d)
    for l in range(DEPTH):
        wg = ffn_w_gate[l].astype(BF16)
        wu = ffn_w_up[l].astype(BF16)
        wd = ffn_w_down[l].astype(BF16)
        w_pad = jnp.where(col_ok[None, :], w_in[l][:, col_src], 0.0).astype(BF16)
        xf = _ffn(xf, mod[l], norm_g[l], wg[0], wu[0], wd[0], fin, sub=0, rows_per_batch=t_, final=False)
        z_pad = _inproj(xf, mod[l], norm_g[l], w_pad, sub=1, rows_per_batch=t_)
        gates_t = z_pad[:, _Z_ML_GATE:_Z_ML_GATE + 2 * ML_HEADS].T
        y_ml = _mlstm(z_pad, gates_t, ml_conv_w[l], ml_conv_b[l], ml_gate_b[l], ml_norm[l], batch=b_, seq=t_)
        y_nsa = _nsa(z_pad, rel_bias, cmp_pos[l], cmp_w1[l], cmp_w2[l], batch=b_, seq=t_)
        y_rw = _rwkv(z_pad, rw_mu[l], rw_w0[l], rw_w_up[l], rw_a0[l], rw_a_up[l], rw_g_up[l], rw_k_k[l],
                     rw_k_a[l], rw_r_k[l], rw_ln[l], batch=b_, seq=t_)
        xf = _outproj(xf, mod[l], y_nsa, y_ml, y_rw, w_out[l].astype(BF16), sub=1, rows_per_batch=t_)
        xf = _ffn(xf, mod[l], norm_g[l], wg[1], wu[1], wd[1], fin, sub=2, rows_per_batch=t_,
                  final=(l == DEPTH - 1))
    return xf.reshape(b_, t_, d)
```

```python
import functools
import math

import numpy as np
import jax
import jax.numpy as jnp
from jax import lax
from jax.experimental import pallas as pl
from jax.experimental.pallas import tpu as pltpu

D_MODEL = 2048
DEPTH = 2
NSA_HEADS = 16
NSA_KV_HEADS = 4
NSA_HEAD_DIM = 64
NSA_GROUP = NSA_HEADS // NSA_KV_HEADS
CMP_LEN = 32
CMP_STRIDE = 16
CMP_HIDDEN = 128
SEL_BLOCK = 64
N_SEL = 16
WINDOW = 512
Q_BLOCK = 128
ML_HEADS = 4
ML_HEAD_DIM = 128
ML_CONV = 4
RW_HEADS = 8
RW_HEAD_DIM = 64
RW_DECAY_RANK = 64
RW_AICL_RANK = 64
RW_GATE_RANK = 128
D_FF = 5632
REL_BUCKETS = 32
REL_MAX_DIST = 128
NORM_EPS = 1e-6
RW_LN_EPS = 64e-5
MASK_NEG = -1e30

NSA_W = NSA_HEADS * NSA_HEAD_DIM
NSA_KV_W = NSA_KV_HEADS * NSA_HEAD_DIM
ML_W = ML_HEADS * ML_HEAD_DIM
RW_W = RW_HEADS * RW_HEAD_DIM
NSA_COLS = NSA_W + 6 * NSA_KV_W + 3 * NSA_HEADS
ML_COLS = 4 * ML_W + 2 * ML_HEADS
RW_COLS = 3 * RW_W + RW_DECAY_RANK + RW_AICL_RANK + RW_GATE_RANK

LANES = 128
SUBLANES = 8
VMEM_LIMIT = 56 * 1024 * 1024

F32 = jnp.float32
BF16 = jnp.bfloat16
HIGHEST = lax.Precision.HIGHEST

_Z_NSA_Q = 0
_Z_NSA_KV = 1024
_Z_NSA_GT = 2560
_Z_ML_GATE = 2688
_Z_RW_WA = 2816
_Z_RW_GD = 2944
_Z_ML = 3072
_Z_RW = 5120
Z_PAD = 6656


def _z_column_map():
    src = np.full((Z_PAD,), -1, np.int64)
    o_ml = NSA_COLS
    o_rw = NSA_COLS + ML_COLS
    src[_Z_NSA_Q:_Z_NSA_Q + NSA_W] = np.arange(NSA_W)
    src[_Z_NSA_KV:_Z_NSA_KV + 6 * NSA_KV_W] = NSA_W + np.arange(6 * NSA_KV_W)
    src[_Z_NSA_GT:_Z_NSA_GT + 3 * NSA_HEADS] = NSA_W + 6 * NSA_KV_W + np.arange(3 * NSA_HEADS)
    src[_Z_ML_GATE:_Z_ML_GATE + 2 * ML_HEADS] = o_ml + 4 * ML_W + np.arange(2 * ML_HEADS)
    src[_Z_RW_WA:_Z_RW_WA + 128] = o_rw + 3 * RW_W + np.arange(128)
    src[_Z_RW_GD:_Z_RW_GD + 128] = o_rw + 3 * RW_W + 128 + np.arange(128)
    src[_Z_ML:_Z_ML + 4 * ML_W] = o_ml + np.arange(4 * ML_W)
    src[_Z_RW:_Z_RW + 3 * RW_W] = o_rw + np.arange(3 * RW_W)
    return src


def _regroup_columns(w):
    src = _z_column_map()
    pieces, start = [], 0
    for i in range(1, Z_PAD + 1):
        run_ends = i == Z_PAD or (src[i] != src[i - 1] + 1 if src[i - 1] >= 0 else src[i] >= 0)
        if run_ends:
            if src[start] >= 0:
                pieces.append(w[:, src[start]:src[start] + i - start])
            else:
                pieces.append(jnp.zeros((w.shape[0], i - start), w.dtype))
            start = i
    return jnp.concatenate(pieces, axis=1)


def _cparams(sem, vmem=VMEM_LIMIT, flags=None):
    return pltpu.CompilerParams(dimension_semantics=sem, vmem_limit_bytes=vmem, flags=flags)


def _sigmoid(x):
    return 1.0 / (1.0 + jnp.exp(-x))


def _silu(x):
    return x * _sigmoid(x)


def _log_sigmoid(x):
    return -(jnp.maximum(-x, 0.0) + jnp.log(1.0 + jnp.exp(-jnp.abs(x))))


def _rms_mod(x, g, shift, scale):
    ms = jnp.mean(x * x, axis=-1, keepdims=True)
    y = x * lax.rsqrt(ms + NORM_EPS) * g
    return y * (1.0 + scale) + shift


def _mod_kernel(c_ref, w_ref, b_ref, o_ref):
    cond = _silu(c_ref[...]).astype(BF16)
    o_ref[...] = jnp.dot(cond, w_ref[...].astype(BF16), preferred_element_type=F32) + b_ref[...]


def _adaln_mod(c, ada_w, ada_b):
    b_, d = c.shape
    depth, _, n = ada_w.shape
    tn = 1024
    c8 = jnp.zeros((SUBLANES, d), F32).at[:b_].set(c)
    out = pl.pallas_call(
        _mod_kernel,
        out_shape=jax.ShapeDtypeStruct((depth, SUBLANES, n), F32),
        grid=(depth, n // tn),
        in_specs=[
            pl.BlockSpec((SUBLANES, d), lambda l, j: (0, 0)),
            pl.BlockSpec((None, d, tn), lambda l, j: (l, 0, j)),
            pl.BlockSpec((None, 1, tn), lambda l, j: (l, 0, j)),
        ],
        out_specs=pl.BlockSpec((None, SUBLANES, tn), lambda l, j: (l, 0, j)),
        compiler_params=_cparams(("parallel", "parallel")),
        name="adaln_mod",
    )(c8, ada_w, ada_b.reshape(depth, 1, n))
    return out[:, :b_].reshape(depth, b_, 9, d)


def _ffn_kernel(x_ref, mod_ref, g_ref, wg_ref, wu_ref, wd_ref, fin_ref, o_ref, h_scr, acc_scr, *, sub, n_f, final):
    f = pl.program_id(1)

    @pl.when(f == 0)
    def _():
        h = _rms_mod(x_ref[...], g_ref[sub:sub + 1, :], mod_ref[3 * sub:3 * sub + 1, :],
                     mod_ref[3 * sub + 1:3 * sub + 2, :])
        h_scr[...] = h.astype(BF16)
        acc_scr[...] = jnp.zeros_like(acc_scr)

    h = h_scr[...]
    gate = jnp.dot(h, wg_ref[...], preferred_element_type=F32)
    up = jnp.dot(h, wu_ref[...], preferred_element_type=F32)
    act = (_silu(gate) * up).astype(BF16)
    acc_scr[...] += jnp.dot(act, wd_ref[...], preferred_element_type=F32)

    @pl.when(f == n_f - 1)
    def _():
        y = x_ref[...] + 0.5 * mod_ref[3 * sub + 2:3 * sub + 3, :] * acc_scr[...]
        if final:
            ms = jnp.mean(y * y, axis=-1, keepdims=True)
            y = y * lax.rsqrt(ms + NORM_EPS) * fin_ref[...]
        o_ref[...] = y


def _ffn(x, mod_l, norm_g_l, wg, wu, wd, final_g, *, sub, rows_per_batch, final):
    m, d = x.shape
    dff = wg.shape[-1]
    tm, tf = 512, 512
    n_f = dff // tf
    bpb = rows_per_batch // tm
    return pl.pallas_call(
        functools.partial(_ffn_kernel, sub=sub, n_f=n_f, final=final),
        out_shape=jax.ShapeDtypeStruct((m, d), F32),
        grid=(m // tm, n_f),
        in_specs=[
            pl.BlockSpec((tm, d), lambda i, f: (i, 0)),
            pl.BlockSpec((None, 9, d), lambda i, f: (i // bpb, 0, 0)),
            pl.BlockSpec((3, d), lambda i, f: (0, 0)),
            pl.BlockSpec((d, tf), lambda i, f: (0, f)),
            pl.BlockSpec((d, tf), lambda i, f: (0, f)),
            pl.BlockSpec((tf, d), lambda i, f: (f, 0)),
            pl.BlockSpec((1, d), lambda i, f: (0, 0)),
        ],
        out_specs=pl.BlockSpec((tm, d), lambda i, f: (i, 0)),
        scratch_shapes=[pltpu.VMEM((tm, d), BF16), pltpu.VMEM((tm, d), F32)],
        compiler_params=_cparams(("parallel", "arbitrary")),
        name="ffn_swiglu",
    )(x, mod_l, norm_g_l, wg, wu, wd, final_g)


def _inproj_kernel(x_ref, mod_ref, g_ref, w_ref, o_ref, h_scr, *, sub):
    @pl.when(pl.program_id(1) == 0)
    def _():
        h = _rms_mod(x_ref[...], g_ref[sub:sub + 1, :], mod_ref[3 * sub:3 * sub + 1, :],
                     mod_ref[3 * sub + 1:3 * sub + 2, :])
        h_scr[...] = h.astype(BF16)

    o_ref[...] = jnp.dot(h_scr[...], w_ref[...], preferred_element_type=F32)


def _inproj(x, mod_l, norm_g_l, w_pad, *, sub, rows_per_batch):
    m, d = x.shape
    n = w_pad.shape[-1]
    tm, tn = 1024, 512
    bpb = rows_per_batch // tm
    return pl.pallas_call(
        functools.partial(_inproj_kernel, sub=sub),
        out_shape=jax.ShapeDtypeStruct((m, n), F32),
        grid=(m // tm, n // tn),
        in_specs=[
            pl.BlockSpec((tm, d), lambda i, j: (i, 0)),
            pl.BlockSpec((None, 9, d), lambda i, j: (i // bpb, 0, 0)),
            pl.BlockSpec((3, d), lambda i, j: (0, 0)),
            pl.BlockSpec((d, tn), lambda i, j: (0, j)),
        ],
        out_specs=pl.BlockSpec((tm, tn), lambda i, j: (i, j)),
        scratch_shapes=[pltpu.VMEM((tm, d), BF16)],
        compiler_params=_cparams(("parallel", "arbitrary")),
        name="in_proj",
    )(x, mod_l, norm_g_l, w_pad)


def _outproj_kernel(x_ref, mod_ref, yn_ref, ym_ref, yr_ref, w_ref, o_ref, *, sub):
    acc = jnp.dot(yn_ref[...].astype(BF16), w_ref[0:NSA_W, :], preferred_element_type=F32)
    acc += jnp.dot(ym_ref[...].astype(BF16), w_ref[NSA_W:NSA_W + ML_W, :], preferred_element_type=F32)
    acc += jnp.dot(yr_ref[...].astype(BF16), w_ref[NSA_W + ML_W:, :], preferred_element_type=F32)
    o_ref[...] = x_ref[...] + mod_ref[3 * sub + 2:3 * sub + 3, :] * acc


def _outproj(x, mod_l, y_nsa, y_ml, y_rw, w_out, *, sub, rows_per_batch):
    m, d = x.shape
    tm = 256
    bpb = rows_per_batch // tm
    return pl.pallas_call(
        functools.partial(_outproj_kernel, sub=sub),
        out_shape=jax.ShapeDtypeStruct((m, d), F32),
        grid=(m // tm,),
        in_specs=[
            pl.BlockSpec((tm, d), lambda i: (i, 0)),
            pl.BlockSpec((None, 9, d), lambda i: (i // bpb, 0, 0)),
            pl.BlockSpec((tm, NSA_W), lambda i: (i, 0)),
            pl.BlockSpec((tm, ML_W), lambda i: (i, 0)),
            pl.BlockSpec((tm, RW_W), lambda i: (i, 0)),
            pl.BlockSpec((d, d), lambda i: (0, 0)),
        ],
        out_specs=pl.BlockSpec((tm, d), lambda i: (i, 0)),
        compiler_params=_cparams(("parallel",)),
        name="out_proj",
    )(x, mod_l, y_nsa, y_ml, y_rw, w_out)


ML_CHUNK = 256


def _shift_rows(x, tail, s, row):
    xs = pltpu.roll(x, s, 0)
    ts = pltpu.roll(tail, s, 0)
    top = jnp.where(row[:SUBLANES] < s, ts, xs[:SUBLANES])
    return jnp.concatenate([top, xs[SUBLANES:]], axis=0)


def _mlstm_kernel(zq_ref, zk_ref, zv_ref, zo_ref, zg_ref, gt_ref, cw_ref, cb_ref, gbc_ref, gbr_ref, ng_ref,
                  o_ref, c_scr, n_scr, m_scr, tq_scr, tk_scr):
    L = zq_ref.shape[0]
    hd = ML_HEAD_DIM

    @pl.when(pl.program_id(1) == 0)
    def _():
        c_scr[...] = jnp.zeros_like(c_scr)
        n_scr[...] = jnp.zeros_like(n_scr)
        m_scr[...] = jnp.zeros_like(m_scr)
        tq_scr[...] = jnp.zeros_like(tq_scr)
        tk_scr[...] = jnp.zeros_like(tk_scr)

    row = lax.broadcasted_iota(jnp.int32, (L, 1), 0)

    def conv(x_ref, tail_scr, w, b):
        x = x_ref[...]
        tail = tail_scr[...]
        acc = x * w[ML_CONV - 1:ML_CONV, :] + b
        for s in range(1, ML_CONV):
            acc = acc + _shift_rows(x, tail, s, row) * w[ML_CONV - 1 - s:ML_CONV - s, :]
        tail_scr[...] = x[L - SUBLANES:, :]
        return _silu(acc)

    cw = cw_ref[...]
    cb = cb_ref[...]
    q_all = conv(zq_ref, tq_scr, cw[:, :ML_W], cb[:, :ML_W])
    k_all = conv(zk_ref, tk_scr, cw[:, ML_W:], cb[:, ML_W:]) * (hd ** -0.5)

    g_col = zg_ref[...] + gbc_ref[...]
    g_row = gt_ref[...] + gbr_ref[...]
    ti = lax.broadcasted_iota(jnp.int32, (L, L), 0)
    si = lax.broadcasted_iota(jnp.int32, (L, L), 1)
    causal = si <= ti
    tril = causal.astype(F32)
    triu = (ti <= si).astype(F32)
    bcum_col = jnp.dot(tril, _log_sigmoid(g_col), precision=HIGHEST, preferred_element_type=F32)
    bcum_row = jnp.dot(_log_sigmoid(g_row), triu, precision=HIGHEST, preferred_element_type=F32)

    for h in range(ML_HEADS):
        sl = slice(h * hd, (h + 1) * hd)
        q = q_all[:, sl]
        k = k_all[:, sl]
        v = zv_ref[:, sl]
        i_col = g_col[:, h:h + 1]
        b_col = bcum_col[:, ML_HEADS + h:ML_HEADS + h + 1]
        i_row = g_row[h:h + 1, :]
        b_row = bcum_row[ML_HEADS + h:ML_HEADS + h + 1, :]
        m_prev = m_scr[h][:, 0:1]
        c_prev = c_scr[h]
        n_prev = n_scr[h]

        log_d = jnp.where(causal, b_col + (i_row - b_row), MASK_NEG)
        m_inter = b_col + m_prev
        m_t = jnp.maximum(jnp.max(log_d, axis=-1, keepdims=True), m_inter)
        dmat = jnp.exp(log_d - m_t)
        qb = q.astype(BF16)
        kb = k.astype(BF16)
        vb = v.astype(BF16)
        s = lax.dot_general(qb, kb, (((1,), (1,)), ((), ())), preferred_element_type=F32) * dmat
        w_inter = jnp.exp(m_inter - m_t)
        num = jnp.dot(s.astype(BF16), vb, preferred_element_type=F32)
        num += w_inter * jnp.dot(qb, c_prev.astype(BF16), preferred_element_type=F32)
        den = jnp.sum(s, axis=-1, keepdims=True) + w_inter * jnp.sum(q * n_prev, axis=-1, keepdims=True)
        hh = num / jnp.maximum(jnp.abs(den), jnp.exp(-m_t))

        g_last = b_row[:, L - 1:L]
        lw_row = g_last - b_row + i_row
        m_new = jnp.maximum(g_last + m_prev, jnp.max(lw_row, axis=-1, keepdims=True))
        wk_col = jnp.exp(g_last - b_col + i_col - m_new)
        dec = jnp.exp(g_last + m_prev - m_new)
        kw = k * wk_col
        c_scr[h] = dec * c_prev + lax.dot_general(kw.astype(BF16), vb, (((0,), (0,)), ((), ())),
                                                  preferred_element_type=F32)
        n_scr[h] = dec * n_prev + jnp.sum(kw, axis=0, keepdims=True)
        m_scr[h] = jnp.broadcast_to(m_new, (1, LANES))

        hh = hh * _sigmoid(zo_ref[:, sl])
        mu = jnp.mean(hh, axis=-1, keepdims=True)
        var = jnp.mean(jnp.square(hh - mu), axis=-1, keepdims=True)
        o_ref[:, sl] = (hh - mu) * lax.rsqrt(var + NORM_EPS) * ng_ref[:, sl]


def _mlstm(z_pad, gates_t, conv_w, conv_b, gate_b, norm_g, *, batch, seq):
    L = min(ML_CHUNK, seq)
    nc = seq // L
    m = batch * seq
    gb = gate_b.reshape(2 * ML_HEADS)
    gb_col = jnp.zeros((1, LANES), F32).at[0, :2 * ML_HEADS].set(gb)
    gb_row = jnp.broadcast_to(gb[:, None], (2 * ML_HEADS, L))
    cb = lambda j: pl.BlockSpec((L, ML_W), lambda b, c: (b * nc + c, _Z_ML // ML_W + j))
    return pl.pallas_call(
        _mlstm_kernel,
        out_shape=jax.ShapeDtypeStruct((m, ML_W), F32),
        grid=(batch, nc),
        in_specs=[
            cb(0), cb(1), cb(2), cb(3),
            pl.BlockSpec((L, LANES), lambda b, c: (b * nc + c, _Z_ML_GATE // LANES)),
            pl.BlockSpec((2 * ML_HEADS, L), lambda b, c: (0, b * nc + c)),
            pl.BlockSpec((ML_CONV, 2 * ML_W), lambda b, c: (0, 0)),
            pl.BlockSpec((1, 2 * ML_W), lambda b, c: (0, 0)),
            pl.BlockSpec((1, LANES), lambda b, c: (0, 0)),
            pl.BlockSpec((2 * ML_HEADS, L), lambda b, c: (0, 0)),
            pl.BlockSpec((1, ML_W), lambda b, c: (0, 0)),
        ],
        out_specs=pl.BlockSpec((L, ML_W), lambda b, c: (b * nc + c, 0)),
        scratch_shapes=[
            pltpu.VMEM((ML_HEADS, ML_HEAD_DIM, ML_HEAD_DIM), F32),
            pltpu.VMEM((ML_HEADS, 1, ML_HEAD_DIM), F32),
            pltpu.VMEM((ML_HEADS, 1, LANES), F32),
            pltpu.VMEM((SUBLANES, ML_W), F32),
            pltpu.VMEM((SUBLANES, ML_W), F32),
        ],
        compiler_params=_cparams(("parallel", "arbitrary")),
        name="mlstm",
    )(z_pad, z_pad, z_pad, z_pad, z_pad, gates_t, conv_w, conv_b.reshape(1, -1), gb_col, gb_row,
      norm_g.reshape(1, -1))


RW_CHUNK = 64
RW_PAIRS = RW_HEADS // 2


_NN = (((1,), (0,)), ((), ()))
_NT = (((1,), (1,)), ((), ()))
_TN = (((0,), (0,)), ((), ()))


def _split(x, parts):
    if isinstance(x, tuple):
        return x
    out = []
    for _ in range(parts - 1):
        h = x.astype(BF16)
        out.append(h)
        x = x - h.astype(F32)
    out.append(x.astype(BF16))
    return tuple(out)


def _dot_split(a, b, dims=_NN, a_parts=2, b_parts=2, order=1):
    a = _split(a, a_parts)
    b = _split(b, b_parts)
    pairs = [(ai, bj) for i, ai in enumerate(a) for j, bj in enumerate(b) if i + j <= order]
    a_cat = jnp.concatenate([p[0] for p in pairs], axis=dims[0][0][0])
    b_cat = jnp.concatenate([p[1] for p in pairs], axis=dims[0][1][0])
    return lax.dot_general(a_cat, b_cat, dims, preferred_element_type=F32)


def _rwkv_kernel(zr_ref, zk_ref, zv_ref, zwa_ref, zgd_ref, mur_ref, muk_ref, muv_ref, muwa_ref, mugd_ref,
                 w0_ref, wup_ref, a0_ref, aup_ref, gup_ref, kk_ref, ka_ref, rk_ref, ln_ref, ones_ref,
                 o_ref, s_scr, tr_scr, tk_scr, tv_scr, twa_scr, tgd_scr):
    C = zr_ref.shape[0]
    hd = RW_HEAD_DIM

    @pl.when(pl.program_id(1) == 0)
    def _():
        s_scr[...] = jnp.zeros_like(s_scr)
        tr_scr[...] = jnp.zeros_like(tr_scr)
        tk_scr[...] = jnp.zeros_like(tk_scr)
        tv_scr[...] = jnp.zeros_like(tv_scr)
        twa_scr[...] = jnp.zeros_like(twa_scr)
        tgd_scr[...] = jnp.zeros_like(tgd_scr)

    row = lax.broadcasted_iota(jnp.int32, (C, 1), 0)

    def shift(x_ref, t_scr, mu_ref):
        x = x_ref[...]
        xp = jnp.where(row == 0, t_scr[...], pltpu.roll(x, 1, 0))
        t_scr[...] = x[C - 1:C, :]
        return x + mu_ref[...] * (xp - x)

    zr = shift(zr_ref, tr_scr, mur_ref)
    zk = shift(zk_ref, tk_scr, muk_ref)
    zv = shift(zv_ref, tv_scr, muv_ref)
    zwa = shift(zwa_ref, twa_scr, muwa_ref)
    zgd = shift(zgd_ref, tgd_scr, mugd_ref)

    ones = (ones_ref[...],)

    def seg_sum(x):
        return _dot_split(x, ones, a_parts=3, order=2)

    u = -(w0_ref[...] + jnp.dot(jnp.tanh(zwa).astype(BF16), wup_ref[...], preferred_element_type=F32))
    softplus = jnp.maximum(u, 0.0) + jnp.log(1.0 + jnp.exp(-jnp.abs(u)))
    logw = -jnp.exp(-softplus - 0.5)
    a = _sigmoid(a0_ref[...] + jnp.dot(zwa.astype(BF16), aup_ref[...], preferred_element_type=F32))
    g = jnp.dot(_sigmoid(zgd).astype(BF16), gup_ref[...], preferred_element_type=F32)
    kk = zk * kk_ref[...]
    k2 = zk * (1.0 + (a - 1.0) * ka_ref[...])
    sums = seg_sum(jnp.concatenate([kk * kk, zr * k2 * rk_ref[...]], axis=0))
    kk = kk / jnp.maximum(jnp.sqrt(sums[:C]), 1e-12)
    bonus_w = sums[C:]
    bv = kk * a

    tc = lax.broadcasted_iota(jnp.int32, (C, C), 0)
    sc = lax.broadcasted_iota(jnp.int32, (C, C), 1)
    tril = ((sc <= tc).astype(BF16),)
    lc = _dot_split(tril, logw, b_parts=3, order=2)
    l_last = lc[C - 1:C, :]
    e_neg = jnp.exp(-lc)
    e_end = jnp.exp(l_last - lc)
    at = -kk * jnp.exp(lc - logw)
    rt = zr * jnp.exp(lc)
    bt = bv * e_neg
    kt = k2 * e_neg
    bte = bv * e_end
    kte = k2 * e_end
    wc = jnp.exp(l_last)

    lane = lax.broadcasted_iota(jnp.int32, (1, LANES), 1)
    head_a = lane < hd
    t2 = lax.broadcasted_iota(jnp.int32, (2 * C, 2 * C), 0)
    s2 = lax.broadcasted_iota(jnp.int32, (2 * C, 2 * C), 1)
    same = (t2 < C) == (s2 < C)
    strict = same & (s2 < t2)
    incl = same & (s2 <= t2)
    eye = (s2 == t2).astype(F32)

    def stack_masked(x):
        return jnp.concatenate([jnp.where(head_a, x, 0.0), jnp.where(head_a, 0.0, x)], axis=0)

    def stack(x):
        return jnp.concatenate([x, x], axis=0)

    n2 = 2 * C
    pairs = range(RW_PAIRS)
    sls = [slice(p * LANES, (p + 1) * LANES) for p in pairs]
    v_s = [stack_masked(zv[:, sl]) for sl in sls]
    s0 = [s_scr[p] for p in pairs]
    gram = [_dot_split(jnp.concatenate([stack_masked(at[:, sl]), stack_masked(rt[:, sl])], axis=0),
                       jnp.concatenate([stack(bt[:, sl]), stack(kt[:, sl]), s0[p]], axis=0), _NT)
            for p, sl in enumerate(sls)]
    a_ab = [jnp.where(strict, gm[:n2, :n2], 0.0) for gm in gram]
    x = [_dot_split(n, n) for n in a_ab]
    tinv = [eye + n for n in a_ab]
    n_sq = int(math.log2(C)) - 1
    for i in range(1, n_sq):
        both = [_dot_split(jnp.concatenate([x[p], tinv[p]], axis=0), x[p]) for p in pairs]
        x = [bo[:n2] for bo in both]
        tinv = [tinv[p] + both[p][n2:] for p in pairs]
    tinv = [tinv[p] + _dot_split(tinv[p], x[p]) for p in pairs]
    rhs = [gram[p][:n2, 2 * n2:] + _dot_split(jnp.where(strict, gram[p][:n2, n2:2 * n2], 0.0), v_s[p])
           for p in pairs]
    uu = [_dot_split(tinv[p], rhs[p]) for p in pairs]
    uv = [_split(jnp.concatenate([uu[p], v_s[p]], axis=0), 2) for p in pairs]
    yy = [gram[p][n2:, 2 * n2:]
          + _dot_split(jnp.concatenate([jnp.where(incl, gram[p][n2:, :n2], 0.0),
                                        jnp.where(incl, gram[p][n2:, n2:2 * n2], 0.0)], axis=1), uv[p])
          for p in pairs]
    ys = [o[:C] + o[C:] for o in yy]
    for p, sl in enumerate(sls):
        ends = jnp.concatenate([stack_masked(bte[:, sl]), stack_masked(kte[:, sl])], axis=0)
        s_scr[p] = s0[p] * wc[:, sl] + _dot_split(uv[p], ends, _TN)

    y = jnp.concatenate(ys, axis=1)
    inv_hd = 1.0 / hd
    mean = seg_sum(y) * inv_hd
    yc = y - mean
    var = seg_sum(yc * yc) * inv_hd
    yn = yc * lax.rsqrt(var + RW_LN_EPS) * ln_ref[0:1, :] + ln_ref[1:2, :]
    o_ref[...] = (yn + bonus_w * zv) * g


def _rwkv(z_pad, mu, w0, w_up, a0, a_up, g_up, k_k, k_a, r_k, ln, *, batch, seq):
    C = RW_CHUNK
    nc = seq // C
    m = batch * seq
    w = RW_W
    row = lambda v: v.reshape(1, -1).astype(F32)
    wup_pad = jnp.zeros((LANES, w), F32).at[:RW_DECAY_RANK].set(w_up).astype(BF16)
    aup_pad = jnp.zeros((LANES, w), F32).at[RW_DECAY_RANK:].set(a_up).astype(BF16)
    head_of = np.arange(w) // RW_HEAD_DIM
    ones = jnp.asarray((head_of[:, None] == head_of[None, :]).astype(np.float32)).astype(BF16)
    zspec = lambda width, off: pl.BlockSpec((C, width), lambda b, c: (b * nc + c, off // width))
    full = lambda shape: pl.BlockSpec(shape, lambda b, c: tuple(0 for _ in shape))
    return pl.pallas_call(
        _rwkv_kernel,
        out_shape=jax.ShapeDtypeStruct((m, w), F32),
        grid=(batch, nc),
        in_specs=[
            zspec(w, _Z_RW), zspec(w, _Z_RW + w), zspec(w, _Z_RW + 2 * w),
            zspec(LANES, _Z_RW_WA), zspec(LANES, _Z_RW_GD),
            full((1, w)), full((1, w)), full((1, w)), full((1, LANES)), full((1, LANES)),
            full((1, w)), full((LANES, w)), full((1, w)), full((LANES, w)), full((LANES, w)),
            full((1, w)), full((1, w)), full((1, w)), full((2, w)), full((w, w)),
        ],
        out_specs=pl.BlockSpec((C, w), lambda b, c: (b * nc + c, 0)),
        scratch_shapes=[
            pltpu.VMEM((RW_PAIRS, LANES, LANES), F32),
            pltpu.VMEM((1, w), F32), pltpu.VMEM((1, w), F32), pltpu.VMEM((1, w), F32),
            pltpu.VMEM((1, LANES), F32), pltpu.VMEM((1, LANES), F32),
        ],
        compiler_params=_cparams(("parallel", "arbitrary")),
        name="rwkv7",
    )(z_pad, z_pad, z_pad, z_pad, z_pad,
      row(mu[:w]), row(mu[w:2 * w]), row(mu[2 * w:3 * w]), row(mu[3 * w:3 * w + LANES]), row(mu[3 * w + LANES:]),
      row(w0), wup_pad, row(a0), aup_pad, g_up.astype(BF16), row(k_k), row(k_a), row(r_k), ln.astype(F32), ones)


CMP_PER_Q = Q_BLOCK // CMP_STRIDE
CMP_PER_SEL = SEL_BLOCK // CMP_STRIDE
CMP_ROWS = 256
QG = NSA_GROUP * Q_BLOCK
FAR_BUCKET = REL_BUCKETS - 1
LOG2E = math.log2(math.e)
NSA_TILE = 2 * SEL_BLOCK
NSA_CHUNK = 4 * NSA_TILE
NSA_V_ROWS = NSA_HEAD_DIM + 16
NSA_UNROLL = 4


def _bucket_table(n_max):
    n = np.arange(n_max)
    max_exact = REL_BUCKETS // 2
    def large(dt):
        v = np.log(np.maximum(n, max_exact).astype(dt) / dt(max_exact)) / dt(math.log(REL_MAX_DIST / max_exact))
        return max_exact + (v * dt(REL_BUCKETS - max_exact)).astype(np.int32)
    l32, l64 = large(np.float32), large(np.float64)
    assert np.array_equal(np.minimum(l32, FAR_BUCKET), np.minimum(l64, FAR_BUCKET))
    return np.where(n < max_exact, n, np.minimum(l64, FAR_BUCKET)).astype(np.int32)


_FAR_DIST = int(np.argmax(_bucket_table(4096) == FAR_BUCKET))
assert np.all(_bucket_table(4096)[_FAR_DIST:] == FAR_BUCKET) and _FAR_DIST <= Q_BLOCK - CMP_STRIDE + 1


def _rel_bias_tile(rel_bias, dist, valid):
    table = _bucket_table(int(dist.max()) + 2)
    idx = table[np.clip(dist, 0, None)]
    tile = rel_bias.astype(F32)[:, idx] * LOG2E
    tile = jnp.where(jnp.asarray(valid)[None], tile, MASK_NEG)
    rows = dist.shape[0]
    tile = tile.reshape(NSA_KV_HEADS, NSA_GROUP, rows, Q_BLOCK).transpose(0, 2, 1, 3)
    return tile.reshape(NSA_KV_HEADS, rows, QG)


def _compress_kernel(r_ref, pos_ref, w1_ref, w2_ref, o_ref):
    nc = r_ref.shape[0]
    half = CMP_STRIDE * NSA_HEAD_DIM
    r = r_ref[...]
    w1 = w1_ref[...].astype(BF16)
    p_top = pos_ref[0:1, :]
    p_bot = pos_ref[1:2, :]
    g1 = jnp.dot((r + p_top).astype(BF16), w1[:half], preferred_element_type=F32)
    g2 = jnp.dot((r + p_bot).astype(BF16), w1[half:], preferred_element_type=F32)
    tail = jnp.dot(jnp.broadcast_to(p_bot, (SUBLANES, half)).astype(BF16), w1[half:],
                   preferred_element_type=F32)[0:1]
    row = lax.broadcasted_iota(jnp.int32, (nc, 1), 0)
    g2_next = jnp.where(row == nc - 1, tail, pltpu.roll(g2, nc - 1, 0))
    hid = _silu(g1 + g2_next)
    o_ref[...] = jnp.dot(hid.astype(BF16), w2_ref[...].astype(BF16), preferred_element_type=F32)


def _compress(r_kv, cmp_pos, cmp_w1, cmp_w2):
    two, bh, nc, width = r_kv.shape
    pos = cmp_pos.reshape(2, 2, width)
    return pl.pallas_call(
        _compress_kernel,
        out_shape=jax.ShapeDtypeStruct((2, bh, nc, NSA_HEAD_DIM), F32),
        grid=(2, bh),
        in_specs=[
            pl.BlockSpec((None, None, nc, width), lambda s, i: (s, i, 0, 0)),
            pl.BlockSpec((None, 2, width), lambda s, i: (s, 0, 0)),
            pl.BlockSpec((None, 2 * width, CMP_HIDDEN), lambda s, i: (s, 0, 0)),
            pl.BlockSpec((None, CMP_HIDDEN, NSA_HEAD_DIM), lambda s, i: (s, 0, 0)),
        ],
        out_specs=pl.BlockSpec((None, None, nc, NSA_HEAD_DIM), lambda s, i: (s, i, 0, 0)),
        compiler_params=_cparams(("parallel", "parallel")),
        name="nsa_compress",
    )(r_kv, pos, cmp_w1, cmp_w2)


def _nsa_cmp_kernel(qt_ref, kc_ref, vct_ref, far_ref, delta_ref, ocmp_ref, sel_ref, flag_ref, s_scr, imp_scr,
                    *, n_sel):
    qb = pl.program_id(2)
    nc = kc_ref.shape[0]
    nb = nc // CMP_PER_SEL
    pad = SUBLANES

    n_vis = CMP_PER_Q * (qb + 1)
    step = min(CMP_ROWS, nc)
    n_var = nc // step

    def attend(width):
        s = jnp.dot(kc_ref[0:width, :].astype(BF16), qt_ref[...], preferred_element_type=F32) + far_ref[...]
        n_idx = (width - step) + lax.broadcasted_iota(jnp.int32, (step, 1), 0)
        s_scr[0:pad, :] = jnp.zeros((pad, QG), F32)
        if width > step:
            s_scr[pad:pad + width - step, :] = s[:width - step]
        s_scr[pad + width - step:pad + width, :] = s[width - step:] + jnp.where(n_idx >= n_vis, MASK_NEG, 0.0)
        win = pl.ds(pl.multiple_of(CMP_PER_Q * qb, SUBLANES), 2 * CMP_PER_Q)
        s_scr[win, :] = s_scr[win, :] + delta_ref[...]
        s = s_scr[pad:pad + width, :]
        m = jnp.max(s, axis=0, keepdims=True)
        offset = jnp.where(m > 0.5 * MASK_NEG, m, -MASK_NEG)
        p = jnp.exp2(s - offset)
        l = jnp.sum(p, axis=0, keepdims=True)
        pn = p * jnp.where(l > 0.0, 1.0 / l, 0.0)
        ocmp_ref[...] = jnp.dot(vct_ref[:, 0:width].astype(BF16), pn.astype(BF16), preferred_element_type=F32)
        imp = pn[:, 0:Q_BLOCK]
        for g in range(1, NSA_GROUP):
            imp = imp + pn[:, g * Q_BLOCK:(g + 1) * Q_BLOCK]
        imp_scr[0:width, :] = imp

        select(width // CMP_PER_SEL)

    def select(nbv):
        parts = [imp_scr[pl.ds(c, nbv, stride=CMP_PER_SEL), :] for c in range(CMP_PER_SEL)]
        score = parts[0]
        for c in range(1, CMP_PER_SEL):
            score = score + parts[c]
        blk = lax.broadcasted_iota(jnp.int32, (nbv, 1), 0)
        score = score + jnp.where(blk == 0, 0.0, pltpu.roll(parts[-1], 1, 0))

        t = qb * Q_BLOCK + lax.broadcasted_iota(jnp.int32, (1, Q_BLOCK), 1)
        cur = t // SEL_BLOCK
        forced = (blk == 0) | (blk == cur) | (blk == cur - 1)
        ok = blk * SEL_BLOCK <= t
        sel = jnp.where(forced, 1.0, 0.0)
        score = jnp.where(forced, -jnp.inf, jnp.where(ok, score, MASK_NEG))
        for _ in range(n_sel - 3):
            top = jnp.max(score, axis=0, keepdims=True)
            first = jnp.min(jnp.where(score == top, blk, nbv), axis=0, keepdims=True)
            hit = blk == first
            sel = jnp.where(hit, 1.0, sel)
            score = jnp.where(hit, -jnp.inf, score)
        sel = jnp.where(ok, sel, 0.0)
        if nbv < nb:
            sel = jnp.concatenate([sel, jnp.zeros((nb - nbv, Q_BLOCK), F32)], axis=0)
        sel_ref[...] = sel
        cnt = lax.dot_general(jnp.ones((SUBLANES, Q_BLOCK), F32), sel, (((1,), (1,)), ((), ())),
                              preferred_element_type=F32)
        flag_ref[...] = (cnt[0:1, :] > 0.0).astype(jnp.int32)

    variant = (n_vis + step - 1) // step
    for v in range(1, n_var + 1):
        pl.when(variant == v)(functools.partial(attend, v * step))


def _nsa_attn_kernel(flag_ref, qt_ref, sel_ref, ocmp_ref, gate_ref, ks_ref, vst_ref, kw_ref, vwt_ref,
                     qaug_ref, bdiag_ref, bwin_ref, o_ref, q_scr, mb_scr, m_scr, acc_scr, sa_scr, sb_scr, sn_scr,
                     sw_scr, p_scr, lst_scr):
    qb = pl.program_id(2)
    hd = NSA_HEAD_DIM
    per_tile = NSA_TILE // SEL_BLOCK
    blocks_per_chunk = NSA_CHUNK // SEL_BLOCK
    tiles_per_chunk = NSA_CHUNK // NSA_TILE

    q_scr[0:hd, :] = qt_ref[...]
    q_scr[hd:, :] = qaug_ref[...]
    mb = (sel_ref[...] - 1.0) * (-MASK_NEG)
    mb_scr[...] = jnp.concatenate([mb] * NSA_GROUP, axis=1)

    def reset():
        m_scr[...] = jnp.full_like(m_scr, MASK_NEG)
        acc_scr[...] = jnp.zeros_like(acc_scr)

    def consume(s_ref, rows, vt):
        blocks = [slice(j * SEL_BLOCK, (j + 1) * SEL_BLOCK) for j in range(len(rows))]
        m_new = m_scr[...]
        m_old = m_new
        for bl, r in zip(blocks, rows):
            m_new = jnp.maximum(m_new, jnp.max(s_ref[bl, :], axis=0, keepdims=True) + r)
        seen = m_new > 0.5 * MASK_NEG
        for bl, r in zip(blocks, rows):
            offset = jnp.where(seen, m_new - r, -MASK_NEG)
            p_scr[bl, :] = jnp.exp2(s_ref[bl, :] - offset).astype(BF16)
        n = len(rows) * SEL_BLOCK
        acc_scr[...] = jnp.exp2(m_old - m_new) * acc_scr[...] + jnp.dot(vt, p_scr[0:n, :],
                                                                    preferred_element_type=F32)
        m_scr[...] = m_new

    def finish():
        return acc_scr[0:hd, :] * (1.0 / acc_scr[hd:hd + 1, :])

    def penalty(ok):
        return jnp.where(ok, 0.0, MASK_NEG)

    near_tiles = WINDOW // NSA_TILE + 1
    near_keys = near_tiles * NSA_TILE
    near_blocks = near_keys // SEL_BLOCK
    near_start = pl.multiple_of(qb * NSA_TILE, NSA_TILE)
    first_near_block = per_tile * (qb - (near_tiles - 1))
    n_far = jnp.maximum(qb - (near_tiles - 1), 0)
    far_blocks = per_tile * n_far
    n_chunks = (n_far + tiles_per_chunk - 1) // tiles_per_chunk

    lst_scr[0] = 0

    def build(c, cnt):
        active = 0
        for j in range(blocks_per_chunk):
            blk = blocks_per_chunk * c + j
            active = active + jnp.where(blk < far_blocks, flag_ref[0, blk], 0)
        lst_scr[cnt] = c
        return cnt + (active > 0).astype(jnp.int32)

    cnt = lax.fori_loop(0, n_chunks, build, 0)

    def chunk_rows(c):
        return pl.multiple_of(c * NSA_CHUNK + WINDOW, NSA_TILE)

    def qk(c, buf):
        buf[...] = jnp.dot(ks_ref[pl.ds(chunk_rows(c), NSA_CHUNK), :], q_scr[...], preferred_element_type=F32)

    def far_consume(c, buf, live):
        blk0 = pl.multiple_of(c * blocks_per_chunk, blocks_per_chunk)
        blk = blk0 + lax.broadcasted_iota(jnp.int32, (blocks_per_chunk, 1), 0)
        rows = mb_scr[pl.ds(blk0, blocks_per_chunk), :] + penalty((blk < far_blocks) & live)
        consume(buf, [rows[j:j + 1] for j in range(blocks_per_chunk)],
                vst_ref[:, pl.ds(chunk_rows(c), NSA_CHUNK)])

    reset()
    qk(lst_scr[0], sa_scr)

    def run_chunks(first, trips, per_trip):
        bufs = (sa_scr, sb_scr)

        def body(j, carry):
            for u in range(per_trip):
                idx = first + per_trip * j + u
                qk(lst_scr[jnp.minimum(idx + 1, cnt - 1)], bufs[(u + 1) % 2])
                far_consume(lst_scr[jnp.minimum(idx, cnt - 1)], bufs[u % 2], idx < cnt)
            return carry

        lax.fori_loop(0, trips, body, 0)

    long_trips = cnt // NSA_UNROLL
    run_chunks(0, long_trips, NSA_UNROLL)
    done = long_trips * NSA_UNROLL
    run_chunks(done, (cnt - done + 1) // 2, 2)

    n_plain = near_keys - 2 * NSA_TILE
    s = jnp.dot(ks_ref[pl.ds(near_start, near_keys), :], q_scr[...], preferred_element_type=F32)
    sn_scr[0:n_plain, :] = s[:n_plain]
    sn_scr[n_plain:, :] = s[n_plain:] + bdiag_ref[...]
    sw_scr[...] = (jnp.dot(kw_ref[pl.ds(near_start, near_keys), :], q_scr[...], preferred_element_type=F32)
                   + bwin_ref[...])
    rows = []
    for j in range(near_blocks):
        blk = first_near_block + j
        rows.append(mb_scr[pl.ds(jnp.maximum(blk, 0), 1), :] + penalty(blk >= 0))
    consume(sn_scr, rows, vst_ref[:, pl.ds(near_start, near_keys)])
    o_slc = finish()

    reset()
    consume(sw_scr, [penalty(first_near_block + j >= 0) for j in range(near_blocks)],
            vwt_ref[:, pl.ds(near_start, near_keys)])
    o_win = finish()

    sig = _sigmoid(gate_ref[...])

    def gate_row(branch):
        return jnp.concatenate([sig[g * 3 + branch:g * 3 + branch + 1, :] for g in range(NSA_GROUP)], axis=1)

    o_t = gate_row(0) * ocmp_ref[...] + gate_row(1) * o_slc + gate_row(2) * o_win
    stacked = jnp.concatenate([o_t[:, g * Q_BLOCK:(g + 1) * Q_BLOCK] for g in range(NSA_GROUP)], axis=0)
    o_ref[...] = stacked.T


def _nsa(z_pad, rel_bias, cmp_pos, cmp_w1, cmp_w2, *, batch, seq):
    kvh, grp, hd = NSA_KV_HEADS, NSA_GROUP, NSA_HEAD_DIM
    bh = batch * kvh
    nqb = seq // Q_BLOCK
    nc = seq // CMP_STRIDE
    nb = seq // SEL_BLOCK
    n_sel = min(N_SEL, nb)
    m = batch * seq

    zq = z_pad[:, _Z_NSA_Q:_Z_NSA_Q + NSA_W].reshape(batch, nqb, Q_BLOCK, kvh, grp, hd)
    qt = (zq * (hd ** -0.5 * LOG2E)).astype(BF16).transpose(0, 3, 1, 5, 4, 2).reshape(bh, nqb, hd, QG)
    kv = z_pad[:, _Z_NSA_KV:_Z_NSA_KV + 6 * NSA_KV_W].reshape(batch, seq, 6, kvh, hd)
    kv = kv.transpose(2, 0, 3, 1, 4).reshape(6, bh, seq, hd)
    r_kv = kv[0:2].reshape(2, bh, nc, CMP_STRIDE * hd)
    def k_aug(k, fold):
        extra = jnp.full((bh, seq, 2), fold, BF16)
        rows = jnp.concatenate([k.astype(BF16), extra, jnp.zeros((bh, seq, LANES - hd - 2), BF16)], axis=-1)
        return jnp.pad(rows, ((0, 0), (WINDOW, 0), (0, 0)))

    def vt_aug(v):
        cols = jnp.concatenate([v.astype(BF16).transpose(0, 2, 1), jnp.ones((bh, 1, seq), BF16),
                                jnp.zeros((bh, NSA_V_ROWS - hd - 1, seq), BF16)], axis=1)
        return jnp.pad(cols, ((0, 0), (0, 0), (WINDOW, 0)))

    ks = k_aug(kv[2], 1.0)
    vst = vt_aug(kv[3])
    kw = k_aug(kv[4], 0.0)
    vwt = vt_aug(kv[5])
    gates_t = z_pad[:, _Z_NSA_GT:_Z_NSA_GT + 3 * NSA_HEADS].reshape(batch, seq, kvh, grp * 3)
    gates_t = gates_t.transpose(0, 2, 3, 1).reshape(bh, grp * 3, seq)

    qi = np.arange(Q_BLOCK)[None, :]
    far = _rel_bias_tile(rel_bias, np.full((1, Q_BLOCK), _FAR_DIST), np.ones((1, Q_BLOCK), bool))
    mrow = np.arange(-CMP_PER_Q, CMP_PER_Q)[:, None]
    d_cmp = qi - CMP_STRIDE * mrow - (CMP_LEN - 1)
    delta = _rel_bias_tile(rel_bias, d_cmp, d_cmp >= 0)
    delta = jnp.where(jnp.asarray(np.tile(d_cmp >= 0, (1, grp)))[None], delta - far, MASK_NEG)
    d_diag = qi + NSA_TILE - np.arange(2 * NSA_TILE)[:, None]
    bdiag = _rel_bias_tile(rel_bias, d_diag, d_diag >= 0) - far
    far_hi = far.astype(BF16)
    far_lo = (far - far_hi.astype(F32)).astype(BF16)
    q_aug = jnp.concatenate([far_hi, far_lo, jnp.zeros((kvh, LANES - hd - 2, QG), BF16)], axis=1)
    d_win = qi + WINDOW - np.arange(WINDOW + NSA_TILE)[:, None]
    bwin = _rel_bias_tile(rel_bias, d_win, (d_win >= 0) & (d_win < WINDOW))

    cmp_kv = _compress(r_kv, cmp_pos, cmp_w1, cmp_w2)
    k_cmp = cmp_kv[0]
    v_cmp_t = cmp_kv[1].transpose(0, 2, 1)

    g3 = (batch, kvh, nqb)
    bhq = lambda b, h, q: (b * kvh + h, q, 0, 0)
    bh0 = lambda b, h, q: (b * kvh + h, 0, 0)
    per_h = lambda b, h, q: (h, 0, 0)
    o_cmp, sel, flags = pl.pallas_call(
        functools.partial(_nsa_cmp_kernel, n_sel=n_sel),
        out_shape=(jax.ShapeDtypeStruct((bh, nqb, hd, QG), F32),
                   jax.ShapeDtypeStruct((bh, nqb, nb, Q_BLOCK), F32),
                   jax.ShapeDtypeStruct((bh, nqb, 1, nb), jnp.int32)),
        grid=g3,
        in_specs=[
            pl.BlockSpec((None, None, hd, QG), bhq),
            pl.BlockSpec((None, nc, hd), bh0),
            pl.BlockSpec((None, hd, nc), bh0),
            pl.BlockSpec((None, 1, QG), per_h),
            pl.BlockSpec((None, 2 * CMP_PER_Q, QG), per_h),
        ],
        out_specs=(pl.BlockSpec((None, None, hd, QG), bhq),
                   pl.BlockSpec((None, None, nb, Q_BLOCK), bhq),
                   pl.BlockSpec((None, None, 1, nb), bhq)),
        scratch_shapes=[pltpu.VMEM((nc + SUBLANES, QG), F32), pltpu.VMEM((nc, Q_BLOCK), F32)],
        compiler_params=_cparams(("parallel", "parallel", "arbitrary")),
        name="nsa_cmp_select",
    )(qt, k_cmp, v_cmp_t, far, delta)

    return pl.pallas_call(
        _nsa_attn_kernel,
        out_shape=jax.ShapeDtypeStruct((m, NSA_W), F32),
        grid=g3,
        in_specs=[
            pl.BlockSpec((None, None, 1, nb), bhq, memory_space=pltpu.SMEM),
            pl.BlockSpec((None, None, hd, QG), bhq),
            pl.BlockSpec((None, None, nb, Q_BLOCK), bhq),
            pl.BlockSpec((None, None, hd, QG), bhq),
            pl.BlockSpec((None, grp * 3, Q_BLOCK), lambda b, h, q: (b * kvh + h, 0, q)),
            pl.BlockSpec((None, WINDOW + seq, LANES), bh0),
            pl.BlockSpec((None, NSA_V_ROWS, WINDOW + seq), bh0),
            pl.BlockSpec((None, WINDOW + seq, LANES), bh0),
            pl.BlockSpec((None, NSA_V_ROWS, WINDOW + seq), bh0),
            pl.BlockSpec((None, LANES - hd, QG), per_h),
            pl.BlockSpec((None, 2 * NSA_TILE, QG), per_h),
            pl.BlockSpec((None, WINDOW + NSA_TILE, QG), per_h),
        ],
        out_specs=pl.BlockSpec((Q_BLOCK, grp * hd), lambda b, h, q: (b * nqb + q, h)),
        scratch_shapes=[pltpu.VMEM((LANES, QG), BF16), pltpu.VMEM((nb, QG), F32), pltpu.VMEM((1, QG), F32),
                        pltpu.VMEM((NSA_V_ROWS, QG), F32), pltpu.VMEM((NSA_CHUNK, QG), F32),
                        pltpu.VMEM((NSA_CHUNK, QG), F32), pltpu.VMEM((WINDOW + NSA_TILE, QG), F32),
                        pltpu.VMEM((WINDOW + NSA_TILE, QG), F32), pltpu.VMEM((WINDOW + NSA_TILE, QG), BF16),
                        pltpu.SMEM((seq // NSA_CHUNK + 1,), jnp.int32)],
        compiler_params=_cparams(("parallel", "parallel", "arbitrary")),
        name="nsa_attention",
    )(flags, qt, sel, o_cmp, gates_t, ks, vst, kw, vwt, q_aug, bdiag, bwin)


def kernel(x, c, rel_bias, final_norm, ada_w, ada_b, norm_g, ffn_w_gate, ffn_w_up, ffn_w_down, w_in, w_out,
           cmp_pos, cmp_w1, cmp_w2, ml_conv_w, ml_conv_b, ml_gate_b, ml_norm, rw_mu, rw_w0, rw_w_up, rw_a0,
           rw_a_up, rw_g_up, rw_k_k, rw_k_a, rw_r_k, rw_ln):
    b_, t_, d = x.shape
    m = b_ * t_
    mod = _adaln_mod(c, ada_w, ada_b)
    fin = final_norm.reshape(1, d)
    xf = x.reshape(m, d)
    for l in range(DEPTH):
        wg = ffn_w_gate[l].astype(BF16)
        wu = ffn_w_up[l].astype(BF16)
        wd = ffn_w_down[l].astype(BF16)
        w_pad = _regroup_columns(w_in[l].astype(BF16))
        xf = _ffn(xf, mod[l], norm_g[l], wg[0], wu[0], wd[0], fin, sub=0, rows_per_batch=t_, final=False)
        z_pad = _inproj(xf, mod[l], norm_g[l], w_pad, sub=1, rows_per_batch=t_)
        gates_t = z_pad[:, _Z_ML_GATE:_Z_ML_GATE + 2 * ML_HEADS].T
        y_ml = _mlstm(z_pad, gates_t, ml_conv_w[l], ml_conv_b[l], ml_gate_b[l], ml_norm[l], batch=b_, seq=t_)
        y_nsa = _nsa(z_pad, rel_bias, cmp_pos[l], cmp_w1[l], cmp_w2[l], batch=b_, seq=t_)
        y_rw = _rwkv(z_pad, rw_mu[l], rw_w0[l], rw_w_up[l], rw_a0[l], rw_a_up[l], rw_g_up[l], rw_k_k[l],
                     rw_k_a[l], rw_r_k[l], rw_ln[l], batch=b_, seq=t_)
        xf = _outproj(xf, mod[l], y_nsa, y_ml, y_rw, w_out[l].astype(BF16), sub=1, rows_per_batch=t_)
        xf = _ffn(xf, mod[l], norm_g[l], wg[1], wu[1], wd[1], fin, sub=2, rows_per_batch=t_,
                  final=(l == DEPTH - 1))
    return xf.reshape(b_, t_, d)
```
